```python
import jax, jax.numpy as jnp
from jax import lax
import numpy as np

D_MODEL = 1024
BATCH = 8
SEQ = 4096
DEPTH = 2

GRID_W = 64
CTX_LEN = 256
BRANCH_W = D_MODEL // 2
N_BRANCH = 4
DN_DK = 128
DN_DV = 128
DN_HEADS = BRANCH_W // DN_DV
DN_CONV = 5
DN_CHUNK = 64
ATT_HD = 64
ATT_HEADS = BRANCH_W // ATT_HD
ATT_KV_HEADS = ATT_HEADS // 4
ATT_GROUP = ATT_HEADS // ATT_KV_HEADS
ATT_BLOCK = 128
ROPE_THETA = 10000.0
CONF_W = BRANCH_W
CONF_K = 31
FN_GROUPS = 4
FN_GW = BRANCH_W // FN_GROUPS
N_EXPERTS = 32
TOP_K = 4
D_EXPERT = D_MODEL
SWIGLU_LIMIT = 7.0
SWIGLU_ALPHA = 1.702
DEEPNORM_ALPHA = (2 * DEPTH) ** 0.25
DEEPNORM_BETA = (8 * DEPTH) ** -0.25
LN_EPS = 1e-6
RMS_EPS = 1e-6
PROJ_SPLITS = (DN_HEADS * DN_DK, DN_HEADS * DN_DK, DN_HEADS * DN_DV, DN_HEADS * DN_DV,
               2 * DN_HEADS, 2 * DN_HEADS,
               ATT_HEADS * ATT_HD, ATT_KV_HEADS * ATT_HD, ATT_KV_HEADS * ATT_HD,
               2 * CONF_W, FN_GROUPS * FN_GW)
PROJ_TOTAL = sum(PROJ_SPLITS)

kernel_name = "hybrid_dit_deltanet_conformer_gqa_fnet_moe"

F32 = jnp.float32


def layer_norm(x, w, b):
    xf = x.astype(F32)
    mu = jnp.mean(xf, axis=-1, keepdims=True)
    var = jnp.mean(jnp.square(xf - mu), axis=-1, keepdims=True)
    y = (xf - mu) * lax.rsqrt(var + LN_EPS)
    return (y * w.astype(F32) + b.astype(F32)).astype(x.dtype)


def rms_norm(x, w):
    xf = x.astype(F32)
    y = xf * lax.rsqrt(jnp.mean(jnp.square(xf), axis=-1, keepdims=True) + RMS_EPS)
    return (y * w.astype(F32)).astype(x.dtype)


def l2_norm(x):
    return x * lax.rsqrt(jnp.sum(jnp.square(x), axis=-1, keepdims=True) + 1e-6)


def adaln(cvec, w, b):
    m = jax.nn.silu(cvec) @ w + b
    m = m.reshape(m.shape[:-1] + (1, 6, D_MODEL))
    return [m[..., j, :] for j in range(6)]


def split_columns(u):
    out, start = [], 0
    for n in PROJ_SPLITS:
        out.append(u[..., start:start + n])
        start += n
    return out


def depthwise_conv(x, w):
    k = w.shape[0]
    pad = (k - 1) // 2
    return lax.conv_general_dilated(x, w[:, None, :], window_strides=(1,), padding=[(pad, pad)],
                                    dimension_numbers=('NWC', 'WIO', 'NWC'),
                                    feature_group_count=w.shape[1])


def dn_features(q, k, v, beta_logit, a_logit, conv_w, a_log, dt_bias):
    B, L, _ = q.shape
    hk = DN_HEADS * DN_DK
    qkv = jax.nn.silu(depthwise_conv(jnp.concatenate([q, k, v], axis=-1), conv_w)).astype(F32)
    qf = l2_norm(qkv[..., :hk].reshape(B, L, DN_HEADS, DN_DK)) * (DN_DK ** -0.5)
    kf = l2_norm(qkv[..., hk:2 * hk].reshape(B, L, DN_HEADS, DN_DK))
    vf = qkv[..., 2 * hk:].reshape(B, L, DN_HEADS, DN_DV)
    beta = jax.nn.sigmoid(beta_logit.astype(F32)).reshape(B, L, 2, DN_HEADS)
    g = -jnp.exp(a_log.astype(F32)) * jax.nn.softplus(
        a_logit.astype(F32).reshape(B, L, 2, DN_HEADS) + dt_bias.astype(F32))
    return qf, kf, vf, beta, g


def gated_delta_chunked(q, k, v, beta, g, s0):
    B, L, H, DK = q.shape
    DV = v.shape[-1]
    C = DN_CHUNK
    N = L // C

    def chunks(t):
        t = t.reshape((B, N, C, H) + t.shape[3:])
        return jnp.moveaxis(t, (1, 3), (0, 2))

    qc, kc, vc = chunks(q), chunks(k), chunks(v)
    bc, gc = chunks(beta), chunks(g)
    gcum = jnp.cumsum(gc, axis=-1)
    idx = jnp.arange(C)
    lower = idx[:, None] >= idx[None, :]
    strict = idx[:, None] > idx[None, :]
    diff = gcum[..., :, None] - gcum[..., None, :]
    decay = jnp.where(lower, jnp.exp(jnp.where(lower, diff, 0.0)), 0.0)
    kk = jnp.einsum('nbhid,nbhjd->nbhij', kc, kc)
    a_mat = jnp.where(strict, kk * decay * bc[..., :, None], 0.0) + jnp.eye(C, dtype=F32)
    rhs = jnp.concatenate([vc * bc[..., None], kc * (bc * jnp.exp(gcum))[..., None]], axis=-1)
    sol = lax.linalg.triangular_solve(a_mat, rhs, left_side=True, lower=True, unit_diagonal=True)
    u, w = sol[..., :DV], sol[..., DV:]
    qk = jnp.einsum('nbhid,nbhjd->nbhij', qc, kc) * decay
    q_dec = qc * jnp.exp(gcum)[..., None]
    g_last = gcum[..., -1]
    k_dec = kc * jnp.exp(g_last[..., None] - gcum)[..., None]

    def step(s, inp):
        u_n, w_n, qk_n, qd_n, kd_n, gl_n = inp
        v_new = u_n - jnp.einsum('bhcd,bhde->bhce', w_n, s)
        o = jnp.einsum('bhcd,bhde->bhce', qd_n, s) + jnp.einsum('bhij,bhje->bhie', qk_n, v_new)
        s = s * jnp.exp(gl_n)[..., None, None] + jnp.einsum('bhcd,bhce->bhde', kd_n, v_new)
        return s, o

    s_fin, o = lax.scan(step, s0, (u, w, qk, q_dec, k_dec, g_last))
    o = jnp.moveaxis(o, (0, 2), (1, 3)).reshape(B, L, H, DV)
    return o, s_fin


def dn_bidirectional(feats, s_fwd, s_bwd):
    q, k, v, beta, g = feats
    o_f, s_f = gated_delta_chunked(q, k, v, beta[:, :, 0], g[:, :, 0], s_fwd)
    flip = lambda t: jnp.flip(t, axis=1)
    o_b, s_b = gated_delta_chunked(flip(q), flip(k), flip(v), flip(beta[:, :, 1]), flip(g[:, :, 1]), s_bwd)
    return o_f + flip(o_b), s_f, s_b


def dn_output(o, z, norm_w):
    B, L = z.shape[:2]
    y = rms_norm(o, norm_w) * jax.nn.silu(z.reshape(B, L, DN_HEADS, DN_DV).astype(F32))
    return y.reshape(B, L, DN_HEADS * DN_DV).astype(z.dtype)


def axial_rope_tables(L):
    rows = L // GRID_W
    row = jnp.repeat(jnp.arange(rows), GRID_W).astype(F32)
    col = jnp.tile(jnp.arange(GRID_W), rows).astype(F32)
    half = ATT_HD // 2
    inv = ROPE_THETA ** (-jnp.arange(0, half, 2, dtype=F32) / half)
    ang = jnp.concatenate([row[:, None] * inv, col[:, None] * inv], axis=-1)
    return jnp.cos(ang), jnp.sin(ang)


def apply_rope(x, cos, sin):
    xr = x.astype(F32).reshape(x.shape[:-1] + (ATT_HD // 2, 2))
    x1, x2 = xr[..., 0], xr[..., 1]
    c, s = cos[:, None, :], sin[:, None, :]
    out = jnp.stack([x1 * c - x2 * s, x1 * s + x2 * c], axis=-1).reshape(x.shape)
    return out.astype(x.dtype)


def attend(q, k, v):
    s = jnp.einsum('bqkgd,bskd->bkgqs', q, k).astype(F32) * (ATT_HD ** -0.5)
    p = jax.nn.softmax(s, axis=-1).astype(v.dtype)
    return jnp.einsum('bkgqs,bskd->bqkgd', p, v)


def attend_blocks(q, k, v):
    B, L = q.shape[:2]
    nb = L // ATT_BLOCK
    qb = jnp.moveaxis(q.reshape((B, nb, ATT_BLOCK) + q.shape[2:]), 1, 0)
    ob = lax.map(lambda qq: attend(qq, k, v), qb)
    return jnp.moveaxis(ob, 0, 1).reshape(B, L, ATT_HEADS * ATT_HD)


def conformer_branch(u, dw_w, dw_b, ln_w, ln_b):
    a, b = u[..., :CONF_W], u[..., CONF_W:]
    y = a * jax.nn.sigmoid(b)
    y = depthwise_conv(y, dw_w) + dw_b
    y = layer_norm(y, ln_w, ln_b)
    return jax.nn.silu(y)


def fourier_branch(u):
    B, L, _ = u.shape
    uf = u.astype(F32).reshape(B, L, FN_GROUPS, FN_GW)
    y = jnp.fft.fftn(uf, axes=(1, 3), norm='ortho').real
    return y.reshape(B, L, FN_GROUPS * FN_GW).astype(u.dtype)


def merge_branches(h, ys, w_branch, w_gate, b_gate, w_out):
    gates = jax.nn.sigmoid(h @ w_gate + b_gate).reshape(h.shape[:-1] + (N_BRANCH, D_MODEL))
    m = gates[..., 0, :] * (ys[0] @ w_branch[0])
    for j in range(1, N_BRANCH):
        m = m + gates[..., j, :] * (ys[j] @ w_branch[j])
    return m @ w_out


def moe(h, router_w, router_b, w_gu, b_gu, w_down, b_down):
    shp = h.shape
    t = h.reshape(-1, D_MODEL)
    logits = (t @ router_w + router_b).astype(F32)
    top_v, top_i = lax.top_k(logits, TOP_K)
    probs = jax.nn.softmax(top_v, axis=-1)
    combine = jnp.sum(jax.nn.one_hot(top_i, N_EXPERTS, dtype=F32) * probs[..., None], axis=-2)
    out = jnp.zeros_like(t)
    for e in range(N_EXPERTS):
        gu = t @ w_gu[e] + b_gu[e]
        gate = jnp.minimum(gu[:, :D_EXPERT], SWIGLU_LIMIT)
        up = jnp.clip(gu[:, D_EXPERT:], -SWIGLU_LIMIT, SWIGLU_LIMIT)
        act = (up + 1.0) * gate * jax.nn.sigmoid(SWIGLU_ALPHA * gate)
        out = out + combine[:, e:e + 1].astype(t.dtype) * (act @ w_down[e] + b_down[e])
    return out.reshape(shp)


def trunk_layer(x, cx, c, c_ctx, lp, rope_cos, rope_sin, last):
    B, L, _ = x.shape
    Lc = cx.shape[1]
    sh1, sc1, g1, sh2, sc2, g2 = adaln(c, lp['w_ada'], lp['b_ada'])
    csh1, csc1, cg1, csh2, csc2, cg2 = adaln(c_ctx, lp['w_ada'], lp['b_ada'])
    h = x * (1.0 + sc1) + sh1
    hc = cx * (1.0 + csc1) + csh1
    dq, dk, dv, dz, dbeta, da, aq, ak, av, cu, fu = split_columns(h @ lp['w_in'])
    cdq, cdk, cdv, cdz, cdbeta, cda, caq, cak, cav, ccu, cfu = split_columns(hc @ lp['w_in'])

    s0 = jnp.zeros((B, DN_HEADS, DN_DK, DN_DV), F32)
    feats_c = dn_features(cdq, cdk, cdv, cdbeta, cda, lp['dn_conv_w'], lp['dn_a_log'], lp['dn_dt_bias'])
    o_c, s_f, s_b = dn_bidirectional(feats_c, s0, s0)
    feats = dn_features(dq, dk, dv, dbeta, da, lp['dn_conv_w'], lp['dn_a_log'], lp['dn_dt_bias'])
    o_l, _, _ = dn_bidirectional(feats, s_f, s_b)
    y_dn = dn_output(o_l, dz, lp['dn_norm_w'])

    k_c = rms_norm(cak.reshape(B, Lc, ATT_KV_HEADS, ATT_HD), lp['att_k_norm_w'])
    v_c = cav.reshape(B, Lc, ATT_KV_HEADS, ATT_HD)
    q_l = apply_rope(rms_norm(aq.reshape(B, L, ATT_HEADS, ATT_HD), lp['att_q_norm_w']), rope_cos, rope_sin)
    k_l = apply_rope(rms_norm(ak.reshape(B, L, ATT_KV_HEADS, ATT_HD), lp['att_k_norm_w']), rope_cos, rope_sin)
    v_l = av.reshape(B, L, ATT_KV_HEADS, ATT_HD)
    k_all = jnp.concatenate([k_c, k_l], axis=1)
    v_all = jnp.concatenate([v_c, v_l], axis=1)
    y_att = attend_blocks(q_l.reshape(B, L, ATT_KV_HEADS, ATT_GROUP, ATT_HD), k_all, v_all)

    y_conf = conformer_branch(cu, lp['conf_dw_w'], lp['conf_dw_b'], lp['conf_ln_w'], lp['conf_ln_b'])
    y_fn = fourier_branch(fu)

    out = merge_branches(h, [y_dn, y_att, y_conf, y_fn], lp['w_branch'], lp['w_gate'], lp['b_gate'], lp['w_out'])
    x = layer_norm(DEEPNORM_ALPHA * x + g1 * out, lp['ln1_w'], lp['ln1_b'])

    h2 = x * (1.0 + sc2) + sh2
    if last:
        f = moe(h2, lp['router_w'], lp['router_b'], lp['exp_w_gu'], lp['exp_b_gu'], lp['exp_w_down'], lp['exp_b_down'])
    else:
        yc_dn = dn_output(o_c, cdz, lp['dn_norm_w'])
        q_c = rms_norm(caq.reshape(B, Lc, ATT_HEADS, ATT_HD), lp['att_q_norm_w'])
        yc_att = attend(q_c.reshape(B, Lc, ATT_KV_HEADS, ATT_GROUP, ATT_HD), k_c, v_c).reshape(B, Lc, ATT_HEADS * ATT_HD)
        yc_conf = conformer_branch(ccu, lp['conf_dw_w'], lp['conf_dw_b'], lp['conf_ln_w'], lp['conf_ln_b'])
        yc_fn = fourier_branch(cfu)
        out_c = merge_branches(hc, [yc_dn, yc_att, yc_conf, yc_fn], lp['w_branch'], lp['w_gate'], lp['b_gate'], lp['w_out'])
        cx = layer_norm(DEEPNORM_ALPHA * cx + cg1 * out_c, lp['ln1_w'], lp['ln1_b'])
        hc2 = cx * (1.0 + csc2) + csh2
        f_all = moe(jnp.concatenate([hc2, h2], axis=1), lp['router_w'], lp['router_b'],
                    lp['exp_w_gu'], lp['exp_b_gu'], lp['exp_w_down'], lp['exp_b_down'])
        f_c, f = f_all[:, :Lc], f_all[:, Lc:]
        cx = layer_norm(DEEPNORM_ALPHA * cx + cg2 * f_c, lp['ln2_w'], lp['ln2_b'])
    x = layer_norm(DEEPNORM_ALPHA * x + g2 * f, lp['ln2_w'], lp['ln2_b'])
    return x, cx


def setup_inputs(seed: int = 0) -> dict:
    key = jax.random.key(seed)
    ks = iter(jax.random.split(key, 40))
    nrm = lambda shape, scale: jax.random.normal(next(ks), shape, F32) * scale
    D = D_MODEL
    dt = jnp.exp(jax.random.uniform(next(ks), (DEPTH, 2, DN_HEADS), F32, np.log(1e-3), np.log(1e-1)))
    return {
        'x': nrm((BATCH, SEQ, D), 1.0),
        'c': nrm((BATCH, D), 1.0),
        'ctx': nrm((BATCH, CTX_LEN, D), 1.0),
        'c_ctx': nrm((D,), 1.0),
        'w_ada': nrm((DEPTH, D, 6 * D), 0.5 * D ** -0.5),
        'b_ada': nrm((DEPTH, 6 * D), 0.02),
        'w_in': nrm((DEPTH, D, PROJ_TOTAL), D ** -0.5),
        'dn_conv_w': nrm((DEPTH, DN_CONV, 2 * DN_HEADS * DN_DK + DN_HEADS * DN_DV), DN_CONV ** -0.5),
        'dn_a_log': jnp.log(jax.random.uniform(next(ks), (DEPTH, 2, DN_HEADS), F32, 1.0, 16.0)),
        'dn_dt_bias': dt + jnp.log(-jnp.expm1(-dt)),
        'dn_norm_w': 1.0 + nrm((DEPTH, DN_DV), 0.1),
        'att_q_norm_w': 1.0 + nrm((DEPTH, ATT_HD), 0.1),
        'att_k_norm_w': 1.0 + nrm((DEPTH, ATT_HD), 0.1),
        'conf_dw_w': nrm((DEPTH, CONF_K, CONF_W), CONF_K ** -0.5),
        'conf_dw_b': nrm((DEPTH, CONF_W), 0.02),
        'conf_ln_w': 1.0 + nrm((DEPTH, CONF_W), 0.1),
        'conf_ln_b': nrm((DEPTH, CONF_W), 0.02),
        'w_branch': nrm((DEPTH, N_BRANCH, BRANCH_W, D), BRANCH_W ** -0.5),
        'w_gate': nrm((DEPTH, D, N_BRANCH * D), D ** -0.5),
        'b_gate': nrm((DEPTH, N_BRANCH * D), 0.02),
        'w_out': nrm((DEPTH, D, D), DEEPNORM_BETA * D ** -0.5),
        'ln1_w': 1.0 + nrm((DEPTH, D), 0.1),
        'ln1_b': nrm((DEPTH, D), 0.02),
        'router_w': nrm((DEPTH, D, N_EXPERTS), D ** -0.5),
        'router_b': nrm((DEPTH, N_EXPERTS), 0.01),
        'exp_w_gu': nrm((DEPTH, N_EXPERTS, D, 2 * D_EXPERT), D ** -0.5),
        'exp_b_gu': nrm((DEPTH, N_EXPERTS, 2 * D_EXPERT), 0.02),
        'exp_w_down': nrm((DEPTH, N_EXPERTS, D_EXPERT, D), DEEPNORM_BETA * D_EXPERT ** -0.5),
        'exp_b_down': nrm((DEPTH, N_EXPERTS, D), 0.02),
        'ln2_w': 1.0 + nrm((DEPTH, D), 0.1),
        'ln2_b': nrm((DEPTH, D), 0.02),
    }


def reference(x, c, ctx, c_ctx, w_ada, b_ada, w_in, dn_conv_w, dn_a_log, dn_dt_bias, dn_norm_w,
              att_q_norm_w, att_k_norm_w, conf_dw_w, conf_dw_b, conf_ln_w, conf_ln_b, w_branch,
              w_gate, b_gate, w_out, ln1_w, ln1_b, router_w, router_b, exp_w_gu, exp_b_gu,
              exp_w_down, exp_b_down, ln2_w, ln2_b):
    rope_cos, rope_sin = axial_rope_tables(x.shape[1])
    cx = ctx
    for i in range(DEPTH):
        lp = dict(w_ada=w_ada[i], b_ada=b_ada[i], w_in=w_in[i], dn_conv_w=dn_conv_w[i],
                  dn_a_log=dn_a_log[i], dn_dt_bias=dn_dt_bias[i], dn_norm_w=dn_norm_w[i],
                  att_q_norm_w=att_q_norm_w[i], att_k_norm_w=att_k_norm_w[i],
                  conf_dw_w=conf_dw_w[i], conf_dw_b=conf_dw_b[i], conf_ln_w=conf_ln_w[i],
                  conf_ln_b=conf_ln_b[i], w_branch=w_branch[i], w_gate=w_gate[i], b_gate=b_gate[i],
                  w_out=w_out[i], ln1_w=ln1_w[i], ln1_b=ln1_b[i], router_w=router_w[i],
                  router_b=router_b[i], exp_w_gu=exp_w_gu[i], exp_b_gu=exp_b_gu[i],
                  exp_w_down=exp_w_down[i], exp_b_down=exp_b_down[i], ln2_w=ln2_w[i], ln2_b=ln2_b[i])
        x, cx = trunk_layer(x, cx, c, c_ctx, lp, rope_cos, rope_sin, i == DEPTH - 1)
    return x
```

```python
import functools

import jax
import jax.numpy as jnp
import numpy as np
from jax import lax
from jax.experimental import pallas as pl
from jax.experimental.pallas import tpu as pltpu

F32 = jnp.float32
BF16 = jnp.bfloat16

D_MODEL = 1024
GRID_W = 64
BRANCH_W = D_MODEL // 2
N_BRANCH = 4
DN_DK = 128
DN_DV = 128
DN_HEADS = BRANCH_W // DN_DV
DN_CHUNK = 64
ATT_HD = 64
ATT_HEADS = BRANCH_W // ATT_HD
ATT_KV_HEADS = ATT_HEADS // 4
ATT_GROUP = ATT_HEADS // ATT_KV_HEADS
ATT_BLOCK = 128
ROPE_THETA = 10000.0
CONF_W = BRANCH_W
FN_GROUPS = 4
FN_GW = BRANCH_W // FN_GROUPS
N_EXPERTS = 32
TOP_K = 4
D_EXPERT = D_MODEL
SWIGLU_LIMIT = 7.0
SWIGLU_ALPHA = 1.702
LN_EPS = 1e-6
RMS_EPS = 1e-6
PROJ_SPLITS = (DN_HEADS * DN_DK, DN_HEADS * DN_DK, DN_HEADS * DN_DV, DN_HEADS * DN_DV,
               2 * DN_HEADS, 2 * DN_HEADS,
               ATT_HEADS * ATT_HD, ATT_KV_HEADS * ATT_HD, ATT_KV_HEADS * ATT_HD,
               2 * CONF_W, FN_GROUPS * FN_GW)

VMEM_LIMIT_BYTES = 48 * 1024 * 1024
MOE_TILE_M = 512


def _moe_ffn_body(tile_expert_ref, x_ref, wgu_ref, bgu_ref, wd_ref, bd_ref, o_ref):
    del tile_expert_ref
    gu = jnp.dot(x_ref[...], wgu_ref[0], preferred_element_type=F32) + bgu_ref[0]
    gate = jnp.minimum(gu[:, :D_EXPERT], SWIGLU_LIMIT)
    up = jnp.clip(gu[:, D_EXPERT:], -SWIGLU_LIMIT, SWIGLU_LIMIT)
    act = (up + 1.0) * gate * jax.nn.sigmoid(SWIGLU_ALPHA * gate)
    y = jnp.dot(act.astype(BF16), wd_ref[0], preferred_element_type=F32) + bd_ref[0]
    o_ref[...] = y


def _moe_ffn(x_sorted, tile_expert, w_gu, b_gu, w_down, b_down):
    p, d = x_sorted.shape
    n_tiles = p // MOE_TILE_M
    grid_spec = pltpu.PrefetchScalarGridSpec(
        num_scalar_prefetch=1,
        grid=(n_tiles,),
        in_specs=[
            pl.BlockSpec((MOE_TILE_M, d), lambda i, te: (i, 0)),
            pl.BlockSpec((1, d, 2 * D_EXPERT), lambda i, te: (te[i], 0, 0)),
            pl.BlockSpec((1, 1, 2 * D_EXPERT), lambda i, te: (te[i], 0, 0)),
            pl.BlockSpec((1, D_EXPERT, d), lambda i, te: (te[i], 0, 0)),
            pl.BlockSpec((1, 1, d), lambda i, te: (te[i], 0, 0)),
        ],
        out_specs=pl.BlockSpec((MOE_TILE_M, d), lambda i, te: (i, 0)),
    )
    return pl.pallas_call(
        _moe_ffn_body,
        grid_spec=grid_spec,
        out_shape=jax.ShapeDtypeStruct((p, d), F32),
        compiler_params=pltpu.CompilerParams(
            dimension_semantics=("arbitrary",), vmem_limit_bytes=VMEM_LIMIT_BYTES),
        name="moe_ffn",
    )(tile_expert, x_sorted, w_gu, b_gu.reshape(N_EXPERTS, 1, -1), w_down,
      b_down.reshape(N_EXPERTS, 1, -1))


def _moe(h, router_w, router_b, w_gu, b_gu, w_down, b_down):
    shp = h.shape
    t = h.reshape(-1, D_MODEL)
    n = t.shape[0]
    a = n * TOP_K
    logits = jnp.dot(t, router_w, precision=lax.Precision.HIGHEST) + router_b
    top_v, top_i = lax.top_k(logits, TOP_K)
    probs = jax.nn.softmax(top_v, axis=-1)

    flat_e = top_i.reshape(a)
    onehot = (flat_e[:, None] == jnp.arange(N_EXPERTS, dtype=flat_e.dtype)[None, :]).astype(jnp.int32)
    csum = jnp.cumsum(onehot, axis=0)
    rank = jnp.take_along_axis(csum, flat_e[:, None], axis=1)[:, 0] - 1
    counts = csum[-1]
    padded = ((counts + MOE_TILE_M - 1) // MOE_TILE_M) * MOE_TILE_M
    group_end = jnp.cumsum(padded)
    group_start = group_end - padded
    pos = group_start[flat_e] + rank
    p_rows = a + N_EXPERTS * MOE_TILE_M
    src_token = jnp.zeros((p_rows,), jnp.int32).at[pos].set(jnp.arange(a, dtype=jnp.int32) // TOP_K)
    tile_start = jnp.arange(p_rows // MOE_TILE_M, dtype=jnp.int32) * MOE_TILE_M
    tile_expert = jnp.minimum(jnp.searchsorted(group_end, tile_start, side='right'),
                              N_EXPERTS - 1).astype(jnp.int32)

    x_sorted = jnp.take(t.astype(BF16), src_token, axis=0)
    y = _moe_ffn(x_sorted, tile_expert, w_gu.astype(BF16), b_gu, w_down.astype(BF16), b_down)
    y_tok = jnp.take(y, pos, axis=0).reshape(n, TOP_K, D_MODEL)
    out = jnp.sum(y_tok * probs[..., None], axis=1)
    return out.reshape(shp)


def _layer_norm(x, w, b):
    mu = jnp.mean(x, axis=-1, keepdims=True)
    var = jnp.mean(jnp.square(x - mu), axis=-1, keepdims=True)
    return (x - mu) * lax.rsqrt(var + LN_EPS) * w + b


def _rms_norm(x, w):
    return x * lax.rsqrt(jnp.mean(jnp.square(x), axis=-1, keepdims=True) + RMS_EPS) * w


def _l2_norm(x):
    return x * lax.rsqrt(jnp.sum(jnp.square(x), axis=-1, keepdims=True) + 1e-6)


def _adaln(cvec, w, b):
    m = jax.nn.silu(cvec) @ w + b
    m = m.reshape(m.shape[:-1] + (1, 6, D_MODEL))
    return [m[..., j, :] for j in range(6)]


def _split_columns(u):
    out, start = [], 0
    for n in PROJ_SPLITS:
        out.append(u[..., start:start + n])
        start += n
    return out


def _depthwise_conv(x, w):
    k = w.shape[0]
    pad = (k - 1) // 2
    return lax.conv_general_dilated(x, w[:, None, :], window_strides=(1,), padding=[(pad, pad)],
                                    dimension_numbers=('NWC', 'WIO', 'NWC'),
                                    feature_group_count=w.shape[1])


def _dn_features(q, k, v, beta_logit, a_logit, conv_w, a_log, dt_bias):
    B, L, _ = q.shape
    hk = DN_HEADS * DN_DK
    qkv = jax.nn.silu(_depthwise_conv(jnp.concatenate([q, k, v], axis=-1), conv_w))
    qf = _l2_norm(qkv[..., :hk].reshape(B, L, DN_HEADS, DN_DK)) * (DN_DK ** -0.5)
    kf = _l2_norm(qkv[..., hk:2 * hk].reshape(B, L, DN_HEADS, DN_DK))
    vf = qkv[..., 2 * hk:].reshape(B, L, DN_HEADS, DN_DV)
    beta = jax.nn.sigmoid(beta_logit).reshape(B, L, 2, DN_HEADS)
    g = -jnp.exp(a_log) * jax.nn.softplus(a_logit.reshape(B, L, 2, DN_HEADS) + dt_bias)
    return qf, kf, vf, beta, g


def _gated_delta_chunked(q, k, v, beta, g, s0):
    B, L, H, DK = q.shape
    DV = v.shape[-1]
    C = DN_CHUNK
    N = L // C

    def chunks(t):
        t = t.reshape((B, N, C, H) + t.shape[3:])
        return jnp.moveaxis(t, (1, 3), (0, 2))

    qc, kc, vc = chunks(q), chunks(k), chunks(v)
    bc, gc = chunks(beta), chunks(g)
    gcum = jnp.cumsum(gc, axis=-1)
    idx = jnp.arange(C)
    lower = idx[:, None] >= idx[None, :]
    strict = idx[:, None] > idx[None, :]
    diff = gcum[..., :, None] - gcum[..., None, :]
    decay = jnp.where(lower, jnp.exp(jnp.where(lower, diff, 0.0)), 0.0)
    kk = jnp.einsum('nbhid,nbhjd->nbhij', kc, kc)
    a_mat = jnp.where(strict, kk * decay * bc[..., :, None], 0.0) + jnp.eye(C, dtype=F32)
    rhs = jnp.concatenate([vc * bc[..., None], kc * (bc * jnp.exp(gcum))[..., None]], axis=-1)
    sol = lax.linalg.triangular_solve(a_mat, rhs, left_side=True, lower=True, unit_diagonal=True)
    u, w = sol[..., :DV], sol[..., DV:]
    qk = jnp.einsum('nbhid,nbhjd->nbhij', qc, kc) * decay
    q_dec = qc * jnp.exp(gcum)[..., None]
    g_last = gcum[..., -1]
    k_dec = kc * jnp.exp(g_last[..., None] - gcum)[..., None]

    def step(s, inp):
        u_n, w_n, qk_n, qd_n, kd_n, gl_n = inp
        v_new = u_n - jnp.einsum('bhcd,bhde->bhce', w_n, s)
        o = jnp.einsum('bhcd,bhde->bhce', qd_n, s) + jnp.einsum('bhij,bhje->bhie', qk_n, v_new)
        s = s * jnp.exp(gl_n)[..., None, None] + jnp.einsum('bhcd,bhce->bhde', kd_n, v_new)
        return s, o

    s_fin, o = lax.scan(step, s0, (u, w, qk, q_dec, k_dec, g_last))
    o = jnp.moveaxis(o, (0, 2), (1, 3)).reshape(B, L, H, DV)
    return o, s_fin


def _dn_bidirectional(feats, s_fwd, s_bwd):
    q, k, v, beta, g = feats
    o_f, s_f = _gated_delta_chunked(q, k, v, beta[:, :, 0], g[:, :, 0], s_fwd)
    flip = lambda t: jnp.flip(t, axis=1)
    o_b, s_b = _gated_delta_chunked(flip(q), flip(k), flip(v), flip(beta[:, :, 1]), flip(g[:, :, 1]), s_bwd)
    return o_f + flip(o_b), s_f, s_b


def _dn_output(o, z, norm_w):
    B, L = z.shape[:2]
    y = _rms_norm(o, norm_w) * jax.nn.silu(z.reshape(B, L, DN_HEADS, DN_DV))
    return y.reshape(B, L, DN_HEADS * DN_DV)


def _axial_rope_tables(L):
    rows = L // GRID_W
    row = jnp.repeat(jnp.arange(rows), GRID_W).astype(F32)
    col = jnp.tile(jnp.arange(GRID_W), rows).astype(F32)
    half = ATT_HD // 2
    inv = ROPE_THETA ** (-jnp.arange(0, half, 2, dtype=F32) / half)
    ang = jnp.concatenate([row[:, None] * inv, col[:, None] * inv], axis=-1)
    return jnp.cos(ang), jnp.sin(ang)


def _apply_rope(x, cos, sin):
    xr = x.reshape(x.shape[:-1] + (ATT_HD // 2, 2))
    x1, x2 = xr[..., 0], xr[..., 1]
    c, s = cos[:, None, :], sin[:, None, :]
    return jnp.stack([x1 * c - x2 * s, x1 * s + x2 * c], axis=-1).reshape(x.shape)


def _attend(q, k, v):
    s = jnp.einsum('bqkgd,bskd->bkgqs', q, k) * (ATT_HD ** -0.5)
    p = jax.nn.softmax(s, axis=-1)
    return jnp.einsum('bkgqs,bskd->bqkgd', p, v)


def _attend_blocks(q, k, v):
    B, L = q.shape[:2]
    nb = L // ATT_BLOCK
    qb = jnp.moveaxis(q.reshape((B, nb, ATT_BLOCK) + q.shape[2:]), 1, 0)
    ob = lax.map(lambda qq: _attend(qq, k, v), qb)
    return jnp.moveaxis(ob, 0, 1).reshape(B, L, ATT_HEADS * ATT_HD)


def _conformer_branch(u, dw_w, dw_b, ln_w, ln_b):
    a, b = u[..., :CONF_W], u[..., CONF_W:]
    y = a * jax.nn.sigmoid(b)
    y = _depthwise_conv(y, dw_w) + dw_b
    return jax.nn.silu(_layer_norm(y, ln_w, ln_b))


def _fourier_branch(u):
    B, L, _ = u.shape
    uf = u.reshape(B, L, FN_GROUPS, FN_GW)
    y = jnp.fft.fftn(uf, axes=(1, 3), norm='ortho').real
    return y.reshape(B, L, FN_GROUPS * FN_GW)


def _merge_branches(h, ys, w_branch, w_gate, b_gate, w_out):
    gates = jax.nn.sigmoid(h @ w_gate + b_gate).reshape(h.shape[:-1] + (N_BRANCH, D_MODEL))
    m = gates[..., 0, :] * (ys[0] @ w_branch[0])
    for j in range(1, N_BRANCH):
        m = m + gates[..., j, :] * (ys[j] @ w_branch[j])
    return m @ w_out


def _trunk_layer(x, cx, c, c_ctx, lp, rope_cos, rope_sin, last, alpha):
    B, L, _ = x.shape
    Lc = cx.shape[1]
    sh1, sc1, g1, sh2, sc2, g2 = _adaln(c, lp['w_ada'], lp['b_ada'])
    csh1, csc1, cg1, csh2, csc2, cg2 = _adaln(c_ctx, lp['w_ada'], lp['b_ada'])
    h = x * (1.0 + sc1) + sh1
    hc = cx * (1.0 + csc1) + csh1
    dq, dk, dv, dz, dbeta, da, aq, ak, av, cu, fu = _split_columns(h @ lp['w_in'])
    cdq, cdk, cdv, cdz, cdbeta, cda, caq, cak, cav, ccu, cfu = _split_columns(hc @ lp['w_in'])

    s0 = jnp.zeros((B, DN_HEADS, DN_DK, DN_DV), F32)
    feats_c = _dn_features(cdq, cdk, cdv, cdbeta, cda, lp['dn_conv_w'], lp['dn_a_log'], lp['dn_dt_bias'])
    o_c, s_f, s_b = _dn_bidirectional(feats_c, s0, s0)
    feats = _dn_features(dq, dk, dv, dbeta, da, lp['dn_conv_w'], lp['dn_a_log'], lp['dn_dt_bias'])
    o_l, _, _ = _dn_bidirectional(feats, s_f, s_b)
    y_dn = _dn_output(o_l, dz, lp['dn_norm_w'])

    k_c = _rms_norm(cak.reshape(B, Lc, ATT_KV_HEADS, ATT_HD), lp['att_k_norm_w'])
    v_c = cav.reshape(B, Lc, ATT_KV_HEADS, ATT_HD)
    q_l = _apply_rope(_rms_norm(aq.reshape(B, L, ATT_HEADS, ATT_HD), lp['att_q_norm_w']), rope_cos, rope_sin)
    k_l = _apply_rope(_rms_norm(ak.reshape(B, L, ATT_KV_HEADS, ATT_HD), lp['att_k_norm_w']), rope_cos, rope_sin)
    v_l = av.reshape(B, L, ATT_KV_HEADS, ATT_HD)
    k_all = jnp.concatenate([k_c, k_l], axis=1)
    v_all = jnp.concatenate([v_c, v_l], axis=1)
    y_att = _attend_blocks(q_l.reshape(B, L, ATT_KV_HEADS, ATT_GROUP, ATT_HD), k_all, v_all)

    y_conf = _conformer_branch(cu, lp['conf_dw_w'], lp['conf_dw_b'], lp['conf_ln_w'], lp['conf_ln_b'])
    y_fn = _fourier_branch(fu)

    out = _merge_branches(h, [y_dn, y_att, y_conf, y_fn], lp['w_branch'], lp['w_gate'], lp['b_gate'], lp['w_out'])
    x = _layer_norm(alpha * x + g1 * out, lp['ln1_w'], lp['ln1_b'])

    h2 = x * (1.0 + sc2) + sh2
    moe_w = (lp['router_w'], lp['router_b'], lp['exp_w_gu'], lp['exp_b_gu'], lp['exp_w_down'], lp['exp_b_down'])
    if last:
        f = _moe(h2, *moe_w)
    else:
        yc_dn = _dn_output(o_c, cdz, lp['dn_norm_w'])
        q_c = _rms_norm(caq.reshape(B, Lc, ATT_HEADS, ATT_HD), lp['att_q_norm_w'])
        yc_att = _attend(q_c.reshape(B, Lc, ATT_KV_HEADS, ATT_GROUP, ATT_HD), k_c, v_c).reshape(B, Lc, ATT_HEADS * ATT_HD)
        yc_conf = _conformer_branch(ccu, lp['conf_dw_w'], lp['conf_dw_b'], lp['conf_ln_w'], lp['conf_ln_b'])
        yc_fn = _fourier_branch(cfu)
        out_c = _merge_branches(hc, [yc_dn, yc_att, yc_conf, yc_fn], lp['w_branch'], lp['w_gate'], lp['b_gate'], lp['w_out'])
        cx = _layer_norm(alpha * cx + cg1 * out_c, lp['ln1_w'], lp['ln1_b'])
        hc2 = cx * (1.0 + csc2) + csh2
        f_all = _moe(jnp.concatenate([hc2, h2], axis=1), *moe_w)
        f_c, f = f_all[:, :Lc], f_all[:, Lc:]
        cx = _layer_norm(alpha * cx + cg2 * f_c, lp['ln2_w'], lp['ln2_b'])
    x = _layer_norm(alpha * x + g2 * f, lp['ln2_w'], lp['ln2_b'])
    return x, cx


def kernel(x, c, ctx, c_ctx, w_ada, b_ada, w_in, dn_conv_w, dn_a_log, dn_dt_bias, dn_norm_w, att_q_norm_w, att_k_norm_w, conf_dw_w, conf_dw_b, conf_ln_w, conf_ln_b, w_branch, w_gate, b_gate, w_out, ln1_w, ln1_b, router_w, router_b, exp_w_gu, exp_b_gu, exp_w_down, exp_b_down, ln2_w, ln2_b):
    depth = w_in.shape[0]
    alpha = (2 * depth) ** 0.25
    rope_cos, rope_sin = _axial_rope_tables(x.shape[1])
    cx = ctx
    for i in range(depth):
        lp = dict(w_ada=w_ada[i], b_ada=b_ada[i], w_in=w_in[i], dn_conv_w=dn_conv_w[i],
                  dn_a_log=dn_a_log[i], dn_dt_bias=dn_dt_bias[i], dn_norm_w=dn_norm_w[i],
                  att_q_norm_w=att_q_norm_w[i], att_k_norm_w=att_k_norm_w[i],
                  conf_dw_w=conf_dw_w[i], conf_dw_b=conf_dw_b[i], conf_ln_w=conf_ln_w[i],
                  conf_ln_b=conf_ln_b[i], w_branch=w_branch[i], w_gate=w_gate[i], b_gate=b_gate[i],
                  w_out=w_out[i], ln1_w=ln1_w[i], ln1_b=ln1_b[i], router_w=router_w[i],
                  router_b=router_b[i], exp_w_gu=exp_w_gu[i], exp_b_gu=exp_b_gu[i],
                  exp_w_down=exp_w_down[i], exp_b_down=exp_b_down[i], ln2_w=ln2_w[i], ln2_b=ln2_b[i])
        x, cx = _trunk_layer(x, cx, c, c_ctx, lp, rope_cos, rope_sin, i == depth - 1, alpha)
    return x
```

```python
import functools

import jax
import jax.numpy as jnp
import numpy as np
from jax import lax
from jax.experimental import pallas as pl
from jax.experimental.pallas import tpu as pltpu

F32 = jnp.float32
BF16 = jnp.bfloat16

D_MODEL = 1024
GRID_W = 64
BRANCH_W = D_MODEL // 2
N_BRANCH = 4
DN_DK = 128
DN_DV = 128
DN_HEADS = BRANCH_W // DN_DV
DN_CONV = 5
DN_CHUNK = 64
ATT_HD = 64
ATT_HEADS = BRANCH_W // ATT_HD
ATT_KV_HEADS = ATT_HEADS // 4
ROPE_THETA = 10000.0
CONF_W = BRANCH_W
CONF_K = 31
FN_GROUPS = 4
FN_GW = BRANCH_W // FN_GROUPS
N_EXPERTS = 32
TOP_K = 4
D_EXPERT = D_MODEL
SWIGLU_LIMIT = 7.0
SWIGLU_ALPHA = 1.702
LN_EPS = 1e-6
RMS_EPS = 1e-6

LANES = 128
HALO = 16
VMEM_LIMIT_BYTES = 52 * 1024 * 1024
MOE_TILE_M = 512
ROW_TILE = 256
ATT_TQ = 256
CONV_ROWS = 32

C_QKV = (0, 1536)
C_Z = (1536, 2048)
C_BA = (2048, 2176)
C_ATT = (2176, 2944)
C_CONF = (2944, 3968)
C_FN = (3968, 4480)
W_IN_COLS = 4480


def _cparams(*sem):
    return pltpu.CompilerParams(dimension_semantics=sem, vmem_limit_bytes=VMEM_LIMIT_BYTES)


def _nt(a, b):
    return lax.dot_general(a, b, (((1,), (1,)), ((), ())), preferred_element_type=F32)


def _tn(a, b):
    return lax.dot_general(a, b, (((0,), (0,)), ((), ())), preferred_element_type=F32)


def _split3(x):
    hi = x.astype(BF16)
    r1 = x - hi.astype(F32)
    mid = r1.astype(BF16)
    lo = (r1 - mid.astype(F32)).astype(BF16)
    return hi, mid, lo


def _dot_exact_lhs(e, x):
    hi, mid, lo = _split3(x)
    return (jnp.dot(e, hi, preferred_element_type=F32) + jnp.dot(e, mid, preferred_element_type=F32)
            + jnp.dot(e, lo, preferred_element_type=F32))


def _dot3(a, b):
    a_hi = a.astype(BF16)
    a_lo = (a - a_hi.astype(F32)).astype(BF16)
    b_hi = b.astype(BF16)
    b_lo = (b - b_hi.astype(F32)).astype(BF16)
    return (jnp.dot(a_hi, b_hi, preferred_element_type=F32) + jnp.dot(a_hi, b_lo, preferred_element_type=F32)
            + jnp.dot(a_lo, b_hi, preferred_element_type=F32))


def _silu(x):
    return x * jax.nn.sigmoid(x)


def _ln_rows(x, w, b):
    mu = jnp.mean(x, axis=-1, keepdims=True)
    xc = x - mu
    var = jnp.mean(xc * xc, axis=-1, keepdims=True)
    return xc * lax.rsqrt(var + LN_EPS) * w + b


def _inproj_body(x_ref, sc_ref, sh_ref, w_ref, wfn_ref, qkv_ref, z_ref, ba_ref, att_ref, conf_ref, pq_ref):
    h = (x_ref[0] * (1.0 + sc_ref[0]) + sh_ref[0]).astype(BF16)

    def proj(c):
        return jnp.dot(h, w_ref[:, c[0]:c[1]], preferred_element_type=F32)

    qkv_ref[0] = proj(C_QKV).astype(BF16)
    z_ref[0] = proj(C_Z).astype(BF16)
    ba_ref[0] = proj(C_BA)
    att_ref[0] = proj(C_ATT).astype(BF16)
    conf_ref[0] = proj(C_CONF).astype(BF16)
    fu = proj(C_FN).astype(BF16)
    pq_ref[0] = jnp.dot(fu, wfn_ref[...], preferred_element_type=F32).astype(BF16)


def _inproj(x, sc, sh, w_cat, w_fn):
    b, l, d = x.shape
    tl = min(ROW_TILE, l)
    widths = [c[1] - c[0] for c in (C_QKV, C_Z, C_BA, C_ATT, C_CONF)] + [2 * BRANCH_W]
    dtypes = [BF16, BF16, F32, BF16, BF16, BF16]
    row = lambda w: pl.BlockSpec((1, tl, w), lambda bi, i: (bi, i, 0))
    vec = pl.BlockSpec((1, 1, d), lambda bi, i: (bi, 0, 0))
    return pl.pallas_call(
        _inproj_body,
        grid=(b, l // tl),
        in_specs=[row(d), vec, vec,
                  pl.BlockSpec((d, W_IN_COLS), lambda bi, i: (0, 0)),
                  pl.BlockSpec((BRANCH_W, 2 * BRANCH_W), lambda bi, i: (0, 0))],
        out_specs=[row(w) for w in widths],
        out_shape=[jax.ShapeDtypeStruct((b, l, w), dt) for w, dt in zip(widths, dtypes)],
        compiler_params=_cparams("parallel", "parallel"),
        name="inproj",
    )(x, sc, sh, w_cat, w_fn)


def _halo_specs(tl, l, c):
    per = tl // HALO
    last = l // HALO - 1
    prev = pl.BlockSpec((1, HALO, c), lambda bi, i: (bi, jnp.maximum(i * per - 1, 0), 0))
    cur = pl.BlockSpec((1, tl, c), lambda bi, i: (bi, i, 0))
    nxt = pl.BlockSpec((1, HALO, c), lambda bi, i: (bi, jnp.minimum((i + 1) * per, last), 0))
    return prev, cur, nxt


def _fill_halo_buffer(buf, prev, cur, nxt, tl):
    i = pl.program_id(1)
    n = pl.num_programs(1)
    buf[HALO:HALO + tl] = cur
    buf[0:HALO] = jnp.where(i > 0, prev, 0.0)
    buf[HALO + tl:2 * HALO + tl] = jnp.where(i < n - 1, nxt, 0.0)


def _depthwise_rows(buf, w_ref, r0, taps, c0, c1):
    pad = (taps - 1) // 2
    acc = None
    for k in range(taps):
        start = HALO + r0 + k - pad
        term = buf[start:start + CONV_ROWS, c0:c1] * w_ref[k:k + 1, c0:c1]
        acc = term if acc is None else acc + term
    return acc


def _conf_body(up_ref, u_ref, un_ref, w_ref, b_ref, lnw_ref, lnb_ref, o_ref, buf):
    tl = u_ref.shape[1]

    def glu(u):
        u = u.astype(F32)
        return u[:, :CONF_W] * jax.nn.sigmoid(u[:, CONF_W:])

    _fill_halo_buffer(buf, glu(up_ref[0]), glu(u_ref[0]), glu(un_ref[0]), tl)
    for r0 in range(0, tl, CONV_ROWS):
        y = _depthwise_rows(buf, w_ref, r0, CONF_K, 0, CONF_W) + b_ref[...]
        o_ref[0, r0:r0 + CONV_ROWS, :] = _silu(_ln_rows(y, lnw_ref[...], lnb_ref[...])).astype(BF16)


def _conformer(u, dw_w, dw_b, ln_w, ln_b):
    b, l, c = u.shape
    tl = min(ROW_TILE, l)
    prev, cur, nxt = _halo_specs(tl, l, c)
    full = lambda a: pl.BlockSpec(a.shape, lambda bi, i: (0,) * a.ndim)
    args = (dw_w, dw_b.reshape(1, -1), ln_w.reshape(1, -1), ln_b.reshape(1, -1))
    return pl.pallas_call(
        _conf_body,
        grid=(b, l // tl),
        in_specs=[prev, cur, nxt] + [full(a) for a in args],
        out_specs=pl.BlockSpec((1, tl, CONF_W), lambda bi, i: (bi, i, 0)),
        out_shape=jax.ShapeDtypeStruct((b, l, CONF_W), BF16),
        scratch_shapes=[pltpu.VMEM((tl + 2 * HALO, CONF_W), F32)],
        compiler_params=_cparams("parallel", "parallel"),
        name="conformer",
    )(u, u, u, *args)


def _dn_prep_body(up_ref, u_ref, un_ref, ba_ref, w_ref, alog_ref, dtb_ref, q_ref, k_ref, v_ref, gb_ref, buf):
    tl = u_ref.shape[1]
    hk = DN_HEADS * DN_DK
    _fill_halo_buffer(buf, up_ref[0].astype(F32), u_ref[0].astype(F32), un_ref[0].astype(F32), tl)
    for r0 in range(0, tl, CONV_ROWS):
        rows = slice(r0, r0 + CONV_ROWS)
        for hd in range(DN_HEADS):
            for part, ref in ((0, q_ref), (1, k_ref), (2, v_ref)):
                c0 = part * hk + hd * DN_DK
                y = _silu(_depthwise_rows(buf, w_ref, r0, DN_CONV, c0, c0 + DN_DK))
                if part < 2:
                    y = y * lax.rsqrt(jnp.sum(y * y, axis=-1, keepdims=True) + 1e-6)
                if part == 0:
                    y = y * (DN_DK ** -0.5)
                ref[0, rows, hd * DN_DK:(hd + 1) * DN_DK] = y
    ba = ba_ref[0]
    t = ba + dtb_ref[...]
    softplus = jnp.maximum(t, 0.0) + jnp.log1p(jnp.exp(-jnp.abs(t)))
    g = -jnp.exp(alog_ref[...]) * softplus
    lane = lax.broadcasted_iota(jnp.int32, ba.shape, 1)
    gb_ref[0] = jnp.where(lane < 2 * DN_HEADS, jax.nn.sigmoid(ba), g)


def _dn_prep(u_qkv, u_ba, conv_w, a_log, dt_bias):
    b, l, c = u_qkv.shape
    tl = min(ROW_TILE, l)
    prev, cur, nxt = _halo_specs(tl, l, c)
    nh = 2 * DN_HEADS
    alog_row = jnp.zeros((1, LANES), F32).at[0, nh:2 * nh].set(a_log.reshape(-1))
    dtb_row = jnp.zeros((1, LANES), F32).at[0, nh:2 * nh].set(dt_bias.reshape(-1))
    row = lambda w: pl.BlockSpec((1, tl, w), lambda bi, i: (bi, i, 0))
    full = lambda a: pl.BlockSpec(a.shape, lambda bi, i: (0,) * a.ndim)
    return pl.pallas_call(
        _dn_prep_body,
        grid=(b, l // tl),
        in_specs=[prev, cur, nxt, row(LANES), full(conv_w), full(alog_row), full(dtb_row)],
        out_specs=[row(BRANCH_W), row(BRANCH_W), row(BRANCH_W), row(LANES)],
        out_shape=[jax.ShapeDtypeStruct((b, l, BRANCH_W), F32)] * 3 + [jax.ShapeDtypeStruct((b, l, LANES), F32)],
        scratch_shapes=[pltpu.VMEM((tl + 2 * HALO, c), F32)],
        compiler_params=_cparams("parallel", "parallel"),
        name="dn_prep",
    )(u_qkv, u_qkv, u_qkv, u_ba, conv_w, alog_row, dtb_row)


def _dn_unit(q, k, v, beta, gc, gr, g_last, lower, strict, eye, s):
    diff = gc - gr
    decay = jnp.where(lower, jnp.exp(jnp.where(lower, diff, 0.0)), 0.0)
    kb = k.astype(BF16)
    kk = _nt(kb, kb)
    n = jnp.where(strict, kk * decay * beta, 0.0)
    t = eye - n
    p = _dot3(n, n)
    for _ in range(4):
        t = t + _dot3(t, p)
        p = _dot3(p, p)
    t = t + _dot3(t, p)
    eg = jnp.exp(gc)
    rhs = jnp.concatenate([v * beta, k * (beta * eg)], axis=-1).astype(BF16)
    sol = jnp.dot(t.astype(BF16), rhs, preferred_element_type=F32)
    u, w = sol[:, :DN_DV], sol[:, DN_DV:]
    qk = _nt(q.astype(BF16), kb) * decay
    sb = s.astype(BF16)
    v_new = u - jnp.dot(w.astype(BF16), sb, preferred_element_type=F32)
    vb = v_new.astype(BF16)
    o = (jnp.dot((q * eg).astype(BF16), sb, preferred_element_type=F32)
         + jnp.dot(qk.astype(BF16), vb, preferred_element_type=F32))
    k_dec = (k * jnp.exp(g_last - gc)).astype(BF16)
    s_new = s * jnp.exp(g_last) + _tn(k_dec, vb)
    return o, s_new


def _dn_scan_body(qf_ref, kf_ref, vf_ref, gbf_ref, qb_ref, kb_ref, vb_ref, gbb_ref, s0f_ref, s0b_ref,
                  of_ref, ob_ref, sf_ref, sb_ref, st_f, st_b):
    c = pl.program_id(1)
    nc = pl.num_programs(1)
    C = DN_CHUNK

    @pl.when(c == 0)
    def _():
        st_f[...] = s0f_ref[0]
        st_b[...] = s0b_ref[0]

    ii = lax.broadcasted_iota(jnp.int32, (C, C), 0)
    jj = lax.broadcasted_iota(jnp.int32, (C, C), 1)
    eye = (ii == jj).astype(F32)
    eye_l = (lax.broadcasted_iota(jnp.int32, (LANES, LANES), 0)
             == lax.broadcasted_iota(jnp.int32, (LANES, LANES), 1)).astype(BF16)
    nh = 2 * DN_HEADS
    for d, (q_ref, k_ref, v_ref, gb_ref, o_ref, st) in enumerate((
            (qf_ref, kf_ref, vf_ref, gbf_ref, of_ref, st_f),
            (qb_ref, kb_ref, vb_ref, gbb_ref, ob_ref, st_b))):
        lower = (ii >= jj) if d == 0 else (ii <= jj)
        strict = (ii > jj) if d == 0 else (ii < jj)
        gb = gb_ref[0]
        gcum = _dot_exact_lhs(lower.astype(BF16), gb)
        hi, mid, lo = _split3(gcum)
        gcum_t = _nt(eye_l, hi) + _nt(eye_l, mid) + _nt(eye_l, lo)
        last = C - 1 if d == 0 else 0
        for hd in range(DN_HEADS):
            cg = nh + d * DN_HEADS + hd
            cb = d * DN_HEADS + hd
            cols = slice(hd * DN_DK, (hd + 1) * DN_DK)
            gc = gcum[:, cg:cg + 1]
            o, s_new = _dn_unit(q_ref[0, :, cols], k_ref[0, :, cols], v_ref[0, :, cols],
                                gb[:, cb:cb + 1], gc, gcum_t[cg:cg + 1, :], gc[last:last + 1, :],
                                lower, strict, eye, st[hd])
            o_ref[0, :, cols] = o
            st[hd] = s_new

    @pl.when(c == nc - 1)
    def _():
        sf_ref[0] = st_f[...]
        sb_ref[0] = st_b[...]


def _dn_scan(qf, kf, vf, gb, s0f, s0b):
    b, l, w = qf.shape
    nc = l // DN_CHUNK
    fwd = lambda cw: pl.BlockSpec((1, DN_CHUNK, cw), lambda bi, c: (bi, c, 0))
    bwd = lambda cw: pl.BlockSpec((1, DN_CHUNK, cw), lambda bi, c: (bi, nc - 1 - c, 0))
    st = pl.BlockSpec((1, DN_HEADS, DN_DK, DN_DV), lambda bi, c: (bi, 0, 0, 0))
    st_shape = jax.ShapeDtypeStruct((b, DN_HEADS, DN_DK, DN_DV), F32)
    return pl.pallas_call(
        _dn_scan_body,
        grid=(b, nc),
        in_specs=[fwd(w), fwd(w), fwd(w), fwd(LANES), bwd(w), bwd(w), bwd(w), bwd(LANES), st, st],
        out_specs=[fwd(w), bwd(w), st, st],
        out_shape=[jax.ShapeDtypeStruct((b, l, w), F32)] * 2 + [st_shape, st_shape],
        scratch_shapes=[pltpu.VMEM((DN_HEADS, DN_DK, DN_DV), F32)] * 2,
        compiler_params=_cparams("parallel", "arbitrary"),
        name="dn_scan",
    )(qf, kf, vf, gb, qf, kf, vf, gb, s0f, s0b)


def _att_prep_body(a_ref, cos_ref, sin_ref, qw_ref, kw_ref, bd_ref, q_ref, kp_ref, vp_ref):
    nq = ATT_HEADS * ATT_HD
    nk = ATT_KV_HEADS * ATT_HD
    a = a_ref[0].astype(F32)

    def norm_rope(x, w, width):
        ms = jnp.dot((x * x).astype(BF16), bd_ref[:width, :width], preferred_element_type=F32)
        y = x * lax.rsqrt(ms + RMS_EPS) * w
        lane = lax.broadcasted_iota(jnp.int32, y.shape, 1)
        half = ATT_HD // 2
        partner = jnp.where(lane % ATT_HD < half, pltpu.roll(y, width - half, 1), pltpu.roll(y, half, 1))
        return y * cos_ref[:, :width] + partner * sin_ref[:, :width]

    q_ref[0] = norm_rope(a[:, :nq], qw_ref[...], nq).astype(BF16)
    k = norm_rope(a[:, nq:nq + nk], kw_ref[...], nk)
    v = a[:, nq + nk:]
    lane = lax.broadcasted_iota(jnp.int32, k.shape, 1)
    first = lane < ATT_HD
    for x, ref in ((k, kp_ref), (v, vp_ref)):
        x0 = jnp.where(first, x, 0.0)
        x1 = jnp.where(first, 0.0, x)
        ref[0, 0] = x0.astype(BF16)
        ref[0, 1] = pltpu.roll(x0, ATT_HD, 1).astype(BF16)
        ref[0, 2] = pltpu.roll(x1, ATT_HD, 1).astype(BF16)
        ref[0, 3] = x1.astype(BF16)


def _att_prep(u_att, cosf, sinf, qw, kw, bd):
    b, l, c = u_att.shape
    tl = min(ROW_TILE, l)
    nq = ATT_HEADS * ATT_HD
    nk = ATT_KV_HEADS * ATT_HD
    full = lambda a: pl.BlockSpec(a.shape, lambda bi, i: (0,) * a.ndim)
    tab = pl.BlockSpec((tl, nq), lambda bi, i: (i, 0))
    placed = pl.BlockSpec((1, 4, tl, nk), lambda bi, i: (bi, 0, i, 0))
    return pl.pallas_call(
        _att_prep_body,
        grid=(b, l // tl),
        in_specs=[pl.BlockSpec((1, tl, c), lambda bi, i: (bi, i, 0)), tab, tab, full(qw), full(kw), full(bd)],
        out_specs=[pl.BlockSpec((1, tl, nq), lambda bi, i: (bi, i, 0)), placed, placed],
        out_shape=[jax.ShapeDtypeStruct((b, l, nq), BF16),
                   jax.ShapeDtypeStruct((b, 4, l, nk), BF16), jax.ShapeDtypeStruct((b, 4, l, nk), BF16)],
        compiler_params=_cparams("parallel", "parallel"),
        name="att_prep",
    )(u_att, cosf, sinf, qw, kw, bd)


def _attn_body(q_ref, k_ref, v_ref, o_ref):
    for m in range(ATT_HEADS // 2):
        q = q_ref[0, :, m * LANES:(m + 1) * LANES]
        kv = (2 * m) // (ATT_HEADS // ATT_KV_HEADS)
        acc = None
        for half in range(2):
            s = _nt(q, k_ref[0, 2 * kv + half])
            p = jnp.exp(s - jnp.max(s, axis=-1, keepdims=True))
            denom = jnp.sum(p, axis=-1, keepdims=True)
            o = jnp.dot(p.astype(BF16), v_ref[0, 2 * kv + half], preferred_element_type=F32) / denom
            acc = o if acc is None else acc + o
        o_ref[0, :, m * LANES:(m + 1) * LANES] = acc.astype(BF16)


def _attention(q, kp, vp):
    b, l, nq = q.shape
    s = kp.shape[2]
    tq = min(ATT_TQ, l)
    kv_spec = pl.BlockSpec((1, 4, s, kp.shape[3]), lambda bi, i: (bi, 0, 0, 0))
    return pl.pallas_call(
        _attn_body,
        grid=(b, l // tq),
        in_specs=[pl.BlockSpec((1, tq, nq), lambda bi, i: (bi, i, 0)), kv_spec, kv_spec],
        out_specs=pl.BlockSpec((1, tq, nq), lambda bi, i: (bi, i, 0)),
        out_shape=jax.ShapeDtypeStruct((b, l, nq), BF16),
        compiler_params=_cparams("parallel", "arbitrary"),
        name="attention",
    )(q, kp, vp)


def _dft_body(cl_ref, sl_ref, pq_ref, o_ref):
    p = pq_ref[0, :, :BRANCH_W]
    q = pq_ref[0, :, BRANCH_W:]
    y = (jnp.dot(cl_ref[...], p, preferred_element_type=F32)
         - jnp.dot(sl_ref[...], q, preferred_element_type=F32))
    o_ref[0] = y.astype(BF16)


def _seq_dft(pq, cl, sl):
    b, l, w2 = pq.shape
    tn = min(512, l)
    tab = pl.BlockSpec((tn, l), lambda i, bi: (i, 0))
    return pl.pallas_call(
        _dft_body,
        grid=(l // tn, b),
        in_specs=[tab, tab, pl.BlockSpec((1, l, w2), lambda i, bi: (bi, 0, 0))],
        out_specs=pl.BlockSpec((1, tn, BRANCH_W), lambda i, bi: (bi, i, 0)),
        out_shape=jax.ShapeDtypeStruct((b, l, BRANCH_W), BF16),
        compiler_params=_cparams("parallel", "parallel"),
        name="seq_dft",
    )(cl, sl, pq)


def _dft_tables(n, scale_dtype=BF16):
    k = (jnp.arange(n, dtype=jnp.int32)[:, None] * jnp.arange(n, dtype=jnp.int32)[None, :]) % n
    ang = k.astype(F32) * (2.0 * np.pi / n)
    s = n ** -0.5
    return (jnp.cos(ang) * s).astype(scale_dtype), (jnp.sin(ang) * s).astype(scale_dtype)


def _channel_dft_weight():
    c, s = _dft_tables(FN_GW, F32)
    eye = jnp.eye(FN_GROUPS, dtype=F32)
    return jnp.concatenate([jnp.kron(eye, c), jnp.kron(eye, s)], axis=1).astype(BF16)


def _merge_body(x_ref, of_ref, ob_ref, z_ref, ya_ref, yc_ref, yf_ref, mod_ref, nw_ref, wg_ref, bg_ref,
                wb_ref, wo_ref, lnw_ref, lnb_ref, rw_ref, rb_ref, x1_ref, h2_ref, lg_ref, *, alpha):
    x = x_ref[0]
    sh1, sc1, g1, sh2, sc2 = (mod_ref[0, j:j + 1, :] for j in range(5))
    h = (x * (1.0 + sc1) + sh1).astype(BF16)
    o = of_ref[0] + ob_ref[0]
    z = z_ref[0].astype(F32)
    parts = []
    for hd in range(DN_HEADS):
        cols = slice(hd * DN_DV, (hd + 1) * DN_DV)
        oh = o[:, cols]
        oh = oh * lax.rsqrt(jnp.mean(oh * oh, axis=-1, keepdims=True) + RMS_EPS) * nw_ref[...]
        parts.append(oh * _silu(z[:, cols]))
    y_dn = jnp.concatenate(parts, axis=-1).astype(BF16)
    ys = (y_dn, ya_ref[0], yc_ref[0], yf_ref[0])
    m = None
    for j in range(N_BRANCH):
        cols = slice(j * D_MODEL, (j + 1) * D_MODEL)
        gate = jax.nn.sigmoid(jnp.dot(h, wg_ref[:, cols], preferred_element_type=F32) + bg_ref[:, cols])
        term = gate * jnp.dot(ys[j], wb_ref[j], preferred_element_type=F32)
        m = term if m is None else m + term
    out = jnp.dot(m.astype(BF16), wo_ref[...], preferred_element_type=F32)
    x1 = _ln_rows(alpha * x + g1 * out, lnw_ref[...], lnb_ref[...])
    x1_ref[0] = x1
    h2 = x1 * (1.0 + sc2) + sh2
    h2_ref[0] = h2.astype(BF16)
    lg_ref[0] = _dot3(h2, rw_ref[...]) + rb_ref[...]


def _merge(x, o_f, o_b, z, y_att, y_conf, y_fn, mod, wp, alpha):
    b, l, d = x.shape
    tl = min(ROW_TILE, l)
    row = lambda w: pl.BlockSpec((1, tl, w), lambda bi, i: (bi, i, 0))
    full = lambda a: pl.BlockSpec(a.shape, lambda bi, i: (0,) * a.ndim)
    weights = (wp['dn_norm_w'], wp['w_gate'], wp['b_gate'], wp['w_branch'], wp['w_out'],
               wp['ln1_w'], wp['ln1_b'], wp['router_w'], wp['router_b'])
    return pl.pallas_call(
        functools.partial(_merge_body, alpha=alpha),
        grid=(b, l // tl),
        in_specs=[row(d), row(BRANCH_W), row(BRANCH_W), row(BRANCH_W), row(BRANCH_W), row(BRANCH_W),
                  row(BRANCH_W), pl.BlockSpec((1, 8, d), lambda bi, i: (bi, 0, 0))]
                 + [full(a) for a in weights],
        out_specs=[row(d), row(d), row(LANES)],
        out_shape=[jax.ShapeDtypeStruct((b, l, d), F32), jax.ShapeDtypeStruct((b, l, d), BF16),
                   jax.ShapeDtypeStruct((b, l, LANES), F32)],
        compiler_params=_cparams("parallel", "parallel"),
        name="merge",
    )(x, o_f, o_b, z, y_att, y_conf, y_fn, mod, *weights)


def _ln2_body(x_ref, f_ref, g_ref, w_ref, b_ref, o_ref, *, alpha):
    o_ref[0] = _ln_rows(alpha * x_ref[0] + g_ref[0] * f_ref[0], w_ref[...], b_ref[...])


def _ln2(x1, f, g2, w, bias, alpha):
    b, l, d = x1.shape
    tl = min(2 * ROW_TILE, l)
    row = pl.BlockSpec((1, tl, d), lambda bi, i: (bi, i, 0))
    vec = pl.BlockSpec((1, 1, d), lambda bi, i: (bi, 0, 0))
    par = pl.BlockSpec((1, d), lambda bi, i: (0, 0))
    return pl.pallas_call(
        functools.partial(_ln2_body, alpha=alpha),
        grid=(b, l // tl),
        in_specs=[row, row, vec, par, par],
        out_specs=row,
        out_shape=jax.ShapeDtypeStruct((b, l, d), F32),
        compiler_params=_cparams("parallel", "parallel"),
        name="ln2",
    )(x1, f, g2, w.reshape(1, d), bias.reshape(1, d))


def _moe_ffn_body(tile_expert_ref, x_ref, wgu_ref, bgu_ref, wd_ref, bd_ref, o_ref):
    del tile_expert_ref
    gu = jnp.dot(x_ref[...], wgu_ref[0], preferred_element_type=F32) + bgu_ref[0]
    gate = jnp.minimum(gu[:, :D_EXPERT], SWIGLU_LIMIT)
    up = jnp.clip(gu[:, D_EXPERT:], -SWIGLU_LIMIT, SWIGLU_LIMIT)
    act = (up + 1.0) * gate * jax.nn.sigmoid(SWIGLU_ALPHA * gate)
    y = jnp.dot(act.astype(BF16), wd_ref[0], preferred_element_type=F32) + bd_ref[0]
    o_ref[...] = y


def _moe_ffn(x_sorted, tile_expert, w_gu, b_gu, w_down, b_down):
    p, d = x_sorted.shape
    n_tiles = p // MOE_TILE_M
    grid_spec = pltpu.PrefetchScalarGridSpec(
        num_scalar_prefetch=1,
        grid=(n_tiles,),
        in_specs=[
            pl.BlockSpec((MOE_TILE_M, d), lambda i, te: (i, 0)),
            pl.BlockSpec((1, d, 2 * D_EXPERT), lambda i, te: (te[i], 0, 0)),
            pl.BlockSpec((1, 1, 2 * D_EXPERT), lambda i, te: (te[i], 0, 0)),
            pl.BlockSpec((1, D_EXPERT, d), lambda i, te: (te[i], 0, 0)),
            pl.BlockSpec((1, 1, d), lambda i, te: (te[i], 0, 0)),
        ],
        out_specs=pl.BlockSpec((MOE_TILE_M, d), lambda i, te: (i, 0)),
    )
    return pl.pallas_call(
        _moe_ffn_body,
        grid_spec=grid_spec,
        out_shape=jax.ShapeDtypeStruct((p, d), F32),
        compiler_params=_cparams("arbitrary"),
        name="moe_ffn",
    )(tile_expert, x_sorted, w_gu, b_gu.reshape(N_EXPERTS, 1, -1), w_down,
      b_down.reshape(N_EXPERTS, 1, -1))


def _moe(t, logits, w_gu, b_gu, w_down, b_down):
    n = t.shape[0]
    a = n * TOP_K
    top_v, top_i = lax.top_k(logits, TOP_K)
    probs = jax.nn.softmax(top_v, axis=-1)

    flat_e = top_i.reshape(a)
    onehot = (flat_e[:, None] == jnp.arange(N_EXPERTS, dtype=flat_e.dtype)[None, :]).astype(jnp.int32)
    csum = jnp.cumsum(onehot, axis=0)
    rank = jnp.take_along_axis(csum, flat_e[:, None], axis=1)[:, 0] - 1
    counts = csum[-1]
    padded = ((counts + MOE_TILE_M - 1) // MOE_TILE_M) * MOE_TILE_M
    group_end = jnp.cumsum(padded)
    group_start = group_end - padded
    pos = group_start[flat_e] + rank
    p_rows = a + N_EXPERTS * MOE_TILE_M
    src_token = jnp.zeros((p_rows,), jnp.int32).at[pos].set(jnp.arange(a, dtype=jnp.int32) // TOP_K)
    tile_start = jnp.arange(p_rows // MOE_TILE_M, dtype=jnp.int32) * MOE_TILE_M
    tile_expert = jnp.minimum(jnp.searchsorted(group_end, tile_start, side='right'),
                              N_EXPERTS - 1).astype(jnp.int32)

    x_sorted = jnp.take(t, src_token, axis=0)
    y = _moe_ffn(x_sorted, tile_expert, w_gu, b_gu, w_down, b_down)
    y_tok = jnp.take(y, pos, axis=0).reshape(n, TOP_K, D_MODEL)
    return jnp.sum(y_tok * probs[..., None], axis=1)


def _head_perm():
    return np.concatenate([np.arange(0, ATT_HD, 2), np.arange(1, ATT_HD, 2)])


def _prep_layer_params(lp):
    w = lp['w_in']
    perm = _head_perm()
    qperm = (np.arange(ATT_HEADS)[:, None] * ATT_HD + perm[None, :]).reshape(-1)
    kperm = (np.arange(ATT_KV_HEADS)[:, None] * ATT_HD + perm[None, :]).reshape(-1)
    o = 0
    cols = {}
    for name, width in (('dn', 2048), ('ba', 16), ('aq', 512), ('ak', 128), ('av', 128), ('cu', 1024), ('fu', 512)):
        cols[name] = w[:, o:o + width]
        o += width
    w_cat = jnp.concatenate([
        cols['dn'], jnp.pad(cols['ba'], ((0, 0), (0, LANES - 16))),
        cols['aq'][:, qperm], cols['ak'][:, kperm], cols['av'], cols['cu'], cols['fu']], axis=1).astype(BF16)
    wp = dict(lp)
    wp['w_cat'] = w_cat
    wp['att_qw'] = (jnp.tile(lp['att_q_norm_w'][perm], ATT_HEADS) * (ATT_HD ** -0.5)).reshape(1, -1)
    wp['att_kw'] = jnp.tile(lp['att_k_norm_w'][perm], ATT_KV_HEADS).reshape(1, -1)
    wp['dn_norm_w'] = lp['dn_norm_w'].reshape(1, -1)
    wp['w_gate'] = lp['w_gate'].astype(BF16)
    wp['b_gate'] = lp['b_gate'].reshape(1, -1)
    wp['w_branch'] = lp['w_branch'].astype(BF16)
    wp['w_out'] = lp['w_out'].astype(BF16)
    wp['ln1_w'] = lp['ln1_w'].reshape(1, -1)
    wp['ln1_b'] = lp['ln1_b'].reshape(1, -1)
    wp['router_w'] = jnp.pad(lp['router_w'], ((0, 0), (0, LANES - N_EXPERTS)))
    wp['router_b'] = jnp.pad(lp['router_b'], (0, LANES - N_EXPERTS)).reshape(1, -1)
    wp['exp_w_gu'] = lp['exp_w_gu'].astype(BF16)
    wp['exp_w_down'] = lp['exp_w_down'].astype(BF16)
    return wp


def _rope_tables(l):
    rows = l // GRID_W
    row = jnp.repeat(jnp.arange(rows), GRID_W).astype(F32)
    col = jnp.tile(jnp.arange(GRID_W), rows).astype(F32)
    half = ATT_HD // 2
    inv = ROPE_THETA ** (-jnp.arange(0, half, 2, dtype=F32) / half)
    ang = jnp.concatenate([row[:, None] * inv, col[:, None] * inv], axis=-1)
    cos, sin = jnp.cos(ang), jnp.sin(ang)
    cosf = jnp.tile(jnp.concatenate([cos, cos], axis=-1), (1, ATT_HEADS))
    sinf = jnp.tile(jnp.concatenate([-sin, sin], axis=-1), (1, ATT_HEADS))
    return cosf, sinf


def _adaln(cvec, w, b):
    m = jax.nn.silu(cvec) @ w + b
    return m.reshape(m.shape[:-1] + (6, D_MODEL))


def _mixers(xs, mod, wp, tabs, s0f, s0b, alpha):
    cosf, sinf, bd, w_fn, cl, sl = tabs
    u_qkv, u_z, u_ba, u_att, u_conf, pq = _inproj(xs, mod[:, 1:2], mod[:, 0:1], wp['w_cat'], w_fn)
    qf, kf, vf, gb = _dn_prep(u_qkv, u_ba, wp['dn_conv_w'], wp['dn_a_log'], wp['dn_dt_bias'])
    o_f, o_b, s_f, s_b = _dn_scan(qf, kf, vf, gb, s0f, s0b)
    q_rot, kp, vp = _att_prep(u_att, cosf, sinf, wp['att_qw'], wp['att_kw'], bd)
    y_conf = _conformer(u_conf, wp['conf_dw_w'], wp['conf_dw_b'], wp['conf_ln_w'], wp['conf_ln_b'])
    y_fn = _seq_dft(pq, cl, sl)
    return dict(o_f=o_f, o_b=o_b, z=u_z, q=q_rot, kp=kp, vp=vp, y_conf=y_conf, y_fn=y_fn, s_f=s_f, s_b=s_b)


def _trunk_layer(x, cx, c, c_ctx, lp, tabs_l, tabs_c, last, alpha):
    b, l, d = x.shape
    lc = cx.shape[1]
    wp = _prep_layer_params(lp)
    mod = _adaln(c, lp['w_ada'], lp['b_ada'])
    mod_c = jnp.broadcast_to(_adaln(c_ctx, lp['w_ada'], lp['b_ada'])[None], (b, 6, d))
    pad8 = lambda m: jnp.pad(m, ((0, 0), (0, 2), (0, 0)))
    s0 = jnp.zeros((b, DN_HEADS, DN_DK, DN_DV), F32)

    mc = _mixers(cx, mod_c, wp, tabs_c, s0, s0, alpha)
    ml = _mixers(x, mod, wp, tabs_l, mc['s_f'], mc['s_b'], alpha)
    kp = jnp.concatenate([mc['kp'], ml['kp']], axis=2)
    vp = jnp.concatenate([mc['vp'], ml['vp']], axis=2)
    y_att = _attention(ml['q'], kp, vp)
    x1, h2, logits = _merge(x, ml['o_f'], ml['o_b'], ml['z'], y_att, ml['y_conf'], ml['y_fn'], pad8(mod), wp, alpha)
    moe_w = (wp['exp_w_gu'], lp['exp_b_gu'], wp['exp_w_down'], lp['exp_b_down'])
    if last:
        f = _moe(h2.reshape(-1, d), logits.reshape(-1, LANES)[:, :N_EXPERTS], *moe_w).reshape(b, l, d)
    else:
        yc_att = _attention(mc['q'], mc['kp'], mc['vp'])
        cx1, hc2, logits_c = _merge(cx, mc['o_f'], mc['o_b'], mc['z'], yc_att, mc['y_conf'], mc['y_fn'],
                                    pad8(mod_c), wp, alpha)
        t_all = jnp.concatenate([hc2.reshape(-1, d), h2.reshape(-1, d)], axis=0)
        lg_all = jnp.concatenate([logits_c.reshape(-1, LANES), logits.reshape(-1, LANES)], axis=0)[:, :N_EXPERTS]
        f_all = _moe(t_all, lg_all, *moe_w)
        f_c = f_all[:b * lc].reshape(b, lc, d)
        f = f_all[b * lc:].reshape(b, l, d)
        cx = _ln2(cx1, f_c, mod_c[:, 5:6], lp['ln2_w'], lp['ln2_b'], alpha)
    x = _ln2(x1, f, mod[:, 5:6], lp['ln2_w'], lp['ln2_b'], alpha)
    return x, cx


def _tables(l, rope):
    nq = ATT_HEADS * ATT_HD
    if rope:
        cosf, sinf = _rope_tables(l)
    else:
        cosf, sinf = jnp.ones((l, nq), F32), jnp.zeros((l, nq), F32)
    bd = jnp.kron(jnp.eye(ATT_HEADS, dtype=F32), jnp.full((ATT_HD, ATT_HD), 1.0 / ATT_HD, F32)).astype(BF16)
    cl, sl = _dft_tables(l)
    return cosf, sinf, bd, _channel_dft_weight(), cl, sl


def kernel(x, c, ctx, c_ctx, w_ada, b_ada, w_in, dn_conv_w, dn_a_log, dn_dt_bias, dn_norm_w, att_q_norm_w, att_k_norm_w, conf_dw_w, conf_dw_b, conf_ln_w, conf_ln_b, w_branch, w_gate, b_gate, w_out, ln1_w, ln1_b, router_w, router_b, exp_w_gu, exp_b_gu, exp_w_down, exp_b_down, ln2_w, ln2_b):
    depth = w_in.shape[0]
    alpha = (2 * depth) ** 0.25
    tabs_l = _tables(x.shape[1], True)
    tabs_c = _tables(ctx.shape[1], False)
    cx = ctx
    for i in range(depth):
        lp = dict(w_ada=w_ada[i], b_ada=b_ada[i], w_in=w_in[i], dn_conv_w=dn_conv_w[i],
                  dn_a_log=dn_a_log[i], dn_dt_bias=dn_dt_bias[i], dn_norm_w=dn_norm_w[i],
                  att_q_norm_w=att_q_norm_w[i], att_k_norm_w=att_k_norm_w[i],
                  conf_dw_w=conf_dw_w[i], conf_dw_b=conf_dw_b[i], conf_ln_w=conf_ln_w[i],
                  conf_ln_b=conf_ln_b[i], w_branch=w_branch[i], w_gate=w_gate[i], b_gate=b_gate[i],
                  w_out=w_out[i], ln1_w=ln1_w[i], ln1_b=ln1_b[i], router_w=router_w[i],
                  router_b=router_b[i], exp_w_gu=exp_w_gu[i], exp_b_gu=exp_b_gu[i],
                  exp_w_down=exp_w_down[i], exp_b_down=exp_b_down[i], ln2_w=ln2_w[i], ln2_b=ln2_b[i])
        x, cx = _trunk_layer(x, cx, c, c_ctx, lp, tabs_l, tabs_c, i == depth - 1, alpha)
    return x
```

```python
import functools
import math

import jax
import jax.numpy as jnp
import numpy as np
from jax import lax
from jax.experimental import pallas as pl
from jax.experimental.pallas import tpu as pltpu

F32 = jnp.float32
BF16 = jnp.bfloat16

D_MODEL = 1024
GRID_W = 64
BRANCH_W = D_MODEL // 2
N_BRANCH = 4
DN_DK = 128
DN_DV = 128
DN_HEADS = BRANCH_W // DN_DV
DN_CONV = 5
DN_CHUNK = 64
ATT_HD = 64
ATT_HEADS = BRANCH_W // ATT_HD
ATT_KV_HEADS = ATT_HEADS // 4
ROPE_THETA = 10000.0
CONF_W = BRANCH_W
CONF_K = 31
FN_GROUPS = 4
FN_GW = BRANCH_W // FN_GROUPS
N_EXPERTS = 32
TOP_K = 4
D_EXPERT = D_MODEL
SWIGLU_LIMIT = 7.0
SWIGLU_ALPHA = 1.702
LN_EPS = 1e-6
RMS_EPS = 1e-6

LANES = 128
HALO = 16
VMEM_LIMIT_BYTES = 52 * 1024 * 1024
MOE_VMEM_LIMIT_BYTES = 58 * 1024 * 1024
MOE_TILE_M = 512
ROW_TILE = 256
ATT_TQ = 256
CONV_ROWS = 32
DN_CHUNKS_PER_STEP = 2

C_QKV = (0, 1536)
C_Z = (1536, 2048)
C_BA = (2048, 2176)
C_ATT = (2176, 2944)
C_CONF = (2944, 3968)
C_FN = (3968, 4480)
W_IN_COLS = 4480


def _cparams(*sem):
    return pltpu.CompilerParams(dimension_semantics=sem, vmem_limit_bytes=VMEM_LIMIT_BYTES)


def _nt(a, b):
    return lax.dot_general(a, b, (((1,), (1,)), ((), ())), preferred_element_type=F32)


def _tn(a, b):
    return lax.dot_general(a, b, (((0,), (0,)), ((), ())), preferred_element_type=F32)


def _split3(x):
    hi = x.astype(BF16)
    r1 = x - hi.astype(F32)
    mid = r1.astype(BF16)
    lo = (r1 - mid.astype(F32)).astype(BF16)
    return hi, mid, lo


def _dot_exact_lhs(e, x):
    hi, mid, lo = _split3(x)
    return (jnp.dot(e, hi, preferred_element_type=F32) + jnp.dot(e, mid, preferred_element_type=F32)
            + jnp.dot(e, lo, preferred_element_type=F32))


def _dot3(a, b):
    a_hi = a.astype(BF16)
    a_lo = (a - a_hi.astype(F32)).astype(BF16)
    b_hi = b.astype(BF16)
    b_lo = (b - b_hi.astype(F32)).astype(BF16)
    return (jnp.dot(a_hi, b_hi, preferred_element_type=F32) + jnp.dot(a_hi, b_lo, preferred_element_type=F32)
            + jnp.dot(a_lo, b_hi, preferred_element_type=F32))


def _silu(x):
    return x * jax.nn.sigmoid(x)


def _ln_rows(x, w, b):
    mu = jnp.mean(x, axis=-1, keepdims=True)
    xc = x - mu
    var = jnp.mean(xc * xc, axis=-1, keepdims=True)
    return xc * lax.rsqrt(var + LN_EPS) * w + b


def _inproj_body(x_ref, sc_ref, sh_ref, w_ref, wfn_ref, qkv_ref, z_ref, ba_ref, att_ref, conf_ref, pq_ref):
    h = (x_ref[0] * (1.0 + sc_ref[0]) + sh_ref[0]).astype(BF16)

    def proj(c):
        return jnp.dot(h, w_ref[:, c[0]:c[1]], preferred_element_type=F32)

    qkv_ref[0] = proj(C_QKV).astype(BF16)
    z_ref[0] = proj(C_Z).astype(BF16)
    ba_ref[0] = proj(C_BA)
    att_ref[0] = proj(C_ATT).astype(BF16)
    conf_ref[0] = proj(C_CONF).astype(BF16)
    fu = proj(C_FN).astype(BF16)
    pq_ref[0] = jnp.dot(fu, wfn_ref[...], preferred_element_type=F32).astype(BF16)


def _inproj(x, sc, sh, w_cat, w_fn):
    b, l, d = x.shape
    tl = min(ROW_TILE, l)
    widths = [c[1] - c[0] for c in (C_QKV, C_Z, C_BA, C_ATT, C_CONF)] + [2 * BRANCH_W]
    dtypes = [BF16, BF16, F32, BF16, BF16, BF16]
    row = lambda w: pl.BlockSpec((1, tl, w), lambda bi, i: (bi, i, 0))
    vec = pl.BlockSpec((1, 1, d), lambda bi, i: (bi, 0, 0))
    return pl.pallas_call(
        _inproj_body,
        grid=(b, l // tl),
        in_specs=[row(d), vec, vec,
                  pl.BlockSpec((d, W_IN_COLS), lambda bi, i: (0, 0)),
                  pl.BlockSpec((BRANCH_W, 2 * BRANCH_W), lambda bi, i: (0, 0))],
        out_specs=[row(w) for w in widths],
        out_shape=[jax.ShapeDtypeStruct((b, l, w), dt) for w, dt in zip(widths, dtypes)],
        compiler_params=_cparams("parallel", "parallel"),
        name="inproj",
    )(x, sc, sh, w_cat, w_fn)


def _halo_specs(tl, l, c):
    per = tl // HALO
    last = l // HALO - 1
    prev = pl.BlockSpec((1, HALO, c), lambda bi, i: (bi, jnp.maximum(i * per - 1, 0), 0))
    cur = pl.BlockSpec((1, tl, c), lambda bi, i: (bi, i, 0))
    nxt = pl.BlockSpec((1, HALO, c), lambda bi, i: (bi, jnp.minimum((i + 1) * per, last), 0))
    return prev, cur, nxt


def _fill_halo_buffer(buf, prev, cur, nxt, tl):
    i = pl.program_id(1)
    n = pl.num_programs(1)
    buf[HALO:HALO + tl] = cur
    buf[0:HALO] = jnp.where(i > 0, prev, 0.0)
    buf[HALO + tl:2 * HALO + tl] = jnp.where(i < n - 1, nxt, 0.0)


def _depthwise_rows(buf, w_ref, r0, taps, c0, c1):
    pad = (taps - 1) // 2
    acc = None
    for k in range(taps):
        start = HALO + r0 + k - pad
        term = buf[start:start + CONV_ROWS, c0:c1] * w_ref[k:k + 1, c0:c1]
        acc = term if acc is None else acc + term
    return acc


def _conf_body(up_ref, u_ref, un_ref, w_ref, b_ref, lnw_ref, lnb_ref, o_ref, buf):
    tl = u_ref.shape[1]

    def glu(u):
        u = u.astype(F32)
        return u[:, :CONF_W] * jax.nn.sigmoid(u[:, CONF_W:])

    _fill_halo_buffer(buf, glu(up_ref[0]), glu(u_ref[0]), glu(un_ref[0]), tl)
    for r0 in range(0, tl, CONV_ROWS):
        y = _depthwise_rows(buf, w_ref, r0, CONF_K, 0, CONF_W) + b_ref[...]
        o_ref[0, r0:r0 + CONV_ROWS, :] = _silu(_ln_rows(y, lnw_ref[...], lnb_ref[...])).astype(BF16)


def _conformer(u, dw_w, dw_b, ln_w, ln_b):
    b, l, c = u.shape
    tl = min(ROW_TILE, l)
    prev, cur, nxt = _halo_specs(tl, l, c)
    full = lambda a: pl.BlockSpec(a.shape, lambda bi, i: (0,) * a.ndim)
    args = (dw_w, dw_b.reshape(1, -1), ln_w.reshape(1, -1), ln_b.reshape(1, -1))
    return pl.pallas_call(
        _conf_body,
        grid=(b, l // tl),
        in_specs=[prev, cur, nxt] + [full(a) for a in args],
        out_specs=pl.BlockSpec((1, tl, CONF_W), lambda bi, i: (bi, i, 0)),
        out_shape=jax.ShapeDtypeStruct((b, l, CONF_W), BF16),
        scratch_shapes=[pltpu.VMEM((tl + 2 * HALO, CONF_W), F32)],
        compiler_params=_cparams("parallel", "parallel"),
        name="conformer",
    )(u, u, u, *args)


def _dn_prep_body(up_ref, u_ref, un_ref, ba_ref, w_ref, alog_ref, dtb_ref, q_ref, k_ref, v_ref, gb_ref, buf):
    tl = u_ref.shape[1]
    hk = DN_HEADS * DN_DK
    _fill_halo_buffer(buf, up_ref[0].astype(F32), u_ref[0].astype(F32), un_ref[0].astype(F32), tl)
    for r0 in range(0, tl, CONV_ROWS):
        rows = slice(r0, r0 + CONV_ROWS)
        for hd in range(DN_HEADS):
            for part, ref in ((0, q_ref), (1, k_ref), (2, v_ref)):
                c0 = part * hk + hd * DN_DK
                y = _silu(_depthwise_rows(buf, w_ref, r0, DN_CONV, c0, c0 + DN_DK))
                if part < 2:
                    y = y * lax.rsqrt(jnp.sum(y * y, axis=-1, keepdims=True) + 1e-6)
                if part == 0:
                    y = y * (DN_DK ** -0.5)
                ref[0, rows, hd * DN_DK:(hd + 1) * DN_DK] = y
    ba = ba_ref[0]
    t = ba + dtb_ref[...]
    softplus = jnp.maximum(t, 0.0) + jnp.log1p(jnp.exp(-jnp.abs(t)))
    g = -jnp.exp(alog_ref[...]) * softplus
    lane = lax.broadcasted_iota(jnp.int32, ba.shape, 1)
    gb_ref[0] = jnp.where(lane < 2 * DN_HEADS, jax.nn.sigmoid(ba), g)


def _dn_prep(u_qkv, u_ba, conv_w, a_log, dt_bias):
    b, l, c = u_qkv.shape
    tl = min(ROW_TILE, l)
    prev, cur, nxt = _halo_specs(tl, l, c)
    nh = 2 * DN_HEADS
    alog_row = jnp.zeros((1, LANES), F32).at[0, nh:2 * nh].set(a_log.reshape(-1))
    dtb_row = jnp.zeros((1, LANES), F32).at[0, nh:2 * nh].set(dt_bias.reshape(-1))
    row = lambda w: pl.BlockSpec((1, tl, w), lambda bi, i: (bi, i, 0))
    full = lambda a: pl.BlockSpec(a.shape, lambda bi, i: (0,) * a.ndim)
    return pl.pallas_call(
        _dn_prep_body,
        grid=(b, l // tl),
        in_specs=[prev, cur, nxt, row(LANES), full(conv_w), full(alog_row), full(dtb_row)],
        out_specs=[row(BRANCH_W), row(BRANCH_W), row(BRANCH_W), row(LANES)],
        out_shape=[jax.ShapeDtypeStruct((b, l, BRANCH_W), F32)] * 3 + [jax.ShapeDtypeStruct((b, l, LANES), F32)],
        scratch_shapes=[pltpu.VMEM((tl + 2 * HALO, c), F32)],
        compiler_params=_cparams("parallel", "parallel"),
        name="dn_prep",
    )(u_qkv, u_qkv, u_qkv, u_ba, conv_w, alog_row, dtb_row)


def _dn_chunk_body(q_ref, k_ref, v_ref, gb_ref, uf_ref, wqf_ref, qktf_ref, glf_ref,
                   ub_ref, wqb_ref, qktb_ref, glb_ref):
    C = DN_CHUNK
    nh = 2 * DN_HEADS
    ii = lax.broadcasted_iota(jnp.int32, (C, C), 0)
    jj = lax.broadcasted_iota(jnp.int32, (C, C), 1)
    eye = (ii == jj).astype(F32)
    eye_l = (lax.broadcasted_iota(jnp.int32, (LANES, LANES), 0)
             == lax.broadcasted_iota(jnp.int32, (LANES, LANES), 1)).astype(BF16)
    lower = ((ii >= jj), (ii <= jj))
    strict = ((ii > jj), (ii < jj))
    outs = ((uf_ref, wqf_ref, qktf_ref, glf_ref), (ub_ref, wqb_ref, qktb_ref, glb_ref))
    n_chunks = q_ref.shape[1] // C

    cum = {}
    for ck in range(n_chunks):
        gb = gb_ref[0, ck * C:(ck + 1) * C, :]
        for d in range(2):
            gcum = _dot_exact_lhs(lower[d].astype(BF16), gb)
            hi, mid, lo = _split3(gcum)
            cum[ck, d] = (gb, gcum, _nt(eye_l, hi) + _nt(eye_l, mid) + _nt(eye_l, lo))
            last = C - 1 if d == 0 else 0
            outs[d][3][0, ck * 8:(ck + 1) * 8, :] = jnp.broadcast_to(gcum[last:last + 1, :], (8, LANES))

    units = [(ck, d, hd) for ck in range(n_chunks) for d in range(2) for hd in range(DN_HEADS)]
    kk, kbs, decay, n = {}, {}, {}, {}
    for ck in range(n_chunks):
        for hd in range(DN_HEADS):
            kb = k_ref[0, ck * C:(ck + 1) * C, hd * DN_DK:(hd + 1) * DN_DK].astype(BF16)
            kbs[ck, hd] = kb
            kk[ck, hd] = _nt(kb, kb)
    for (ck, d, hd) in units:
        gb, gcum, gcum_t = cum[ck, d]
        cg = nh + d * DN_HEADS + hd
        cb = d * DN_HEADS + hd
        diff = gcum[:, cg:cg + 1] - gcum_t[cg:cg + 1, :]
        dec = jnp.where(lower[d], jnp.exp(jnp.where(lower[d], diff, 0.0)), 0.0)
        decay[ck, d, hd] = dec
        n[ck, d, hd] = jnp.where(strict[d], kk[ck, hd] * dec * gb[:, cb:cb + 1], 0.0)

    t = {u: eye - n[u] for u in units}
    p = {u: _dot3(n[u], n[u]) for u in units}
    for _ in range(4):
        t = {u: t[u] + _dot3(t[u], p[u]) for u in units}
        p = {u: _dot3(p[u], p[u]) for u in units}
    t = {u: t[u] + _dot3(t[u], p[u]) for u in units}

    for (ck, d, hd) in units:
        gb, gcum, _ = cum[ck, d]
        u_ref, wq_ref, qkt_ref, _ = outs[d]
        cg = nh + d * DN_HEADS + hd
        cb = d * DN_HEADS + hd
        rows = slice(ck * C, (ck + 1) * C)
        cols = slice(hd * DN_DK, (hd + 1) * DN_DK)
        q = q_ref[0, rows, cols]
        k = k_ref[0, rows, cols]
        v = v_ref[0, rows, cols]
        beta = gb[:, cb:cb + 1]
        gc = gcum[:, cg:cg + 1]
        last = C - 1 if d == 0 else 0
        eg = jnp.exp(gc)
        rhs = jnp.concatenate([v * beta, k * (beta * eg)], axis=-1).astype(BF16)
        sol = jnp.dot(t[ck, d, hd].astype(BF16), rhs, preferred_element_type=F32)
        u_ref[0, hd, rows, :] = sol[:, :DN_DV]
        wq_ref[0, hd, ck * 2 * C:ck * 2 * C + C, :] = sol[:, DN_DV:].astype(BF16)
        wq_ref[0, hd, ck * 2 * C + C:(ck + 1) * 2 * C, :] = (q * eg).astype(BF16)
        qk = _nt(q.astype(BF16), kbs[ck, hd]) * decay[ck, d, hd]
        k_dec = (k * jnp.exp(gc[last:last + 1, :] - gc)).astype(BF16)
        qkt_ref[0, hd, ck * 3 * C:ck * 3 * C + C, :] = qk.astype(BF16)
        qkt_ref[0, hd, ck * 3 * C + C:(ck + 1) * 3 * C, :] = _nt(eye_l, k_dec).astype(BF16)


def _dn_chunk(qf, kf, vf, gb):
    b, l, w = qf.shape
    C = DN_CHUNK
    rows = min(DN_CHUNKS_PER_STEP * C, l)
    per = rows // C
    nc = l // C
    row = lambda cw: pl.BlockSpec((1, rows, cw), lambda bi, i: (bi, i, 0))
    head = lambda r, cw: pl.BlockSpec((1, DN_HEADS, r, cw), lambda bi, i: (bi, 0, i, 0))
    one_dir_specs = [head(rows, DN_DV), head(2 * rows, DN_DK), head(3 * rows, C),
                     pl.BlockSpec((1, per * 8, LANES), lambda bi, i: (bi, i, 0))]
    one_dir_shapes = [jax.ShapeDtypeStruct((b, DN_HEADS, l, DN_DV), F32),
                      jax.ShapeDtypeStruct((b, DN_HEADS, 2 * l, DN_DK), BF16),
                      jax.ShapeDtypeStruct((b, DN_HEADS, 3 * l, C), BF16),
                      jax.ShapeDtypeStruct((b, nc * 8, LANES), F32)]
    return pl.pallas_call(
        _dn_chunk_body,
        grid=(b, l // rows),
        in_specs=[row(w), row(w), row(w), row(LANES)],
        out_specs=one_dir_specs * 2,
        out_shape=one_dir_shapes * 2,
        compiler_params=_cparams("parallel", "parallel"),
        name="dn_chunk",
    )(qf, kf, vf, gb)


def _dn_rec_body(uf_ref, wqf_ref, qktf_ref, glf_ref, ub_ref, wqb_ref, qktb_ref, glb_ref, s0f_ref, s0b_ref,
                 of_ref, ob_ref, sf_ref, sb_ref, st_f, st_b):
    c = pl.program_id(1)
    nc = pl.num_programs(1)
    C = DN_CHUNK
    nh = 2 * DN_HEADS

    @pl.when(c == 0)
    def _():
        st_f[...] = s0f_ref[0]
        st_b[...] = s0b_ref[0]

    dirs = ((uf_ref, wqf_ref, qktf_ref, glf_ref, of_ref, st_f), (ub_ref, wqb_ref, qktb_ref, glb_ref, ob_ref, st_b))
    units = [(d, hd) for d in range(2) for hd in range(DN_HEADS)]
    s = {(d, hd): dirs[d][5][hd] for (d, hd) in units}
    r1 = {(d, hd): jnp.dot(dirs[d][1][0, hd], s[d, hd].astype(BF16), preferred_element_type=F32)
          for (d, hd) in units}
    vb = {(d, hd): (dirs[d][0][0, hd] - r1[d, hd][:C]).astype(BF16) for (d, hd) in units}
    r2 = {(d, hd): jnp.dot(dirs[d][2][0, hd], vb[d, hd], preferred_element_type=F32) for (d, hd) in units}
    for (d, hd) in units:
        cg = nh + d * DN_HEADS + hd
        decay_all = jnp.exp(dirs[d][3][0, 0:1, cg:cg + 1])
        dirs[d][4][0, :, hd * DN_DV:(hd + 1) * DN_DV] = r1[d, hd][C:] + r2[d, hd][:C]
        dirs[d][5][hd] = s[d, hd] * decay_all + r2[d, hd][C:]

    @pl.when(c == nc - 1)
    def _():
        sf_ref[0] = st_f[...]
        sb_ref[0] = st_b[...]


def _dn_rec(chunk_outs, s0f, s0b):
    uf = chunk_outs[0]
    b, _, l, _ = uf.shape
    C = DN_CHUNK
    nc = l // C
    fwd = lambda c: c
    bwd = lambda c: nc - 1 - c

    def dir_specs(cidx):
        head = lambda r, cw: pl.BlockSpec((1, DN_HEADS, r, cw), lambda bi, c: (bi, 0, cidx(c), 0))
        return [head(C, DN_DV), head(2 * C, DN_DK), head(3 * C, C),
                pl.BlockSpec((1, 8, LANES), lambda bi, c: (bi, cidx(c), 0))]

    st = pl.BlockSpec((1, DN_HEADS, DN_DK, DN_DV), lambda bi, c: (bi, 0, 0, 0))
    st_shape = jax.ShapeDtypeStruct((b, DN_HEADS, DN_DK, DN_DV), F32)
    out = lambda cidx: pl.BlockSpec((1, C, BRANCH_W), lambda bi, c: (bi, cidx(c), 0))
    return pl.pallas_call(
        _dn_rec_body,
        grid=(b, nc),
        in_specs=dir_specs(fwd) + dir_specs(bwd) + [st, st],
        out_specs=[out(fwd), out(bwd), st, st],
        out_shape=[jax.ShapeDtypeStruct((b, l, BRANCH_W), F32)] * 2 + [st_shape, st_shape],
        scratch_shapes=[pltpu.VMEM((DN_HEADS, DN_DK, DN_DV), F32)] * 2,
        compiler_params=_cparams("parallel", "arbitrary"),
        name="dn_rec",
    )(*chunk_outs, s0f, s0b)


def _att_prep_body(a_ref, cos_ref, sin_ref, qw_ref, kw_ref, bd_ref, q_ref, kp_ref, vp_ref):
    nq = ATT_HEADS * ATT_HD
    nk = ATT_KV_HEADS * ATT_HD
    a = a_ref[0].astype(F32)

    def norm_rope(x, w, width):
        ms = jnp.dot((x * x).astype(BF16), bd_ref[:width, :width], preferred_element_type=F32)
        y = x * lax.rsqrt(ms + RMS_EPS) * w
        lane = lax.broadcasted_iota(jnp.int32, y.shape, 1)
        half = ATT_HD // 2
        partner = jnp.where(lane % ATT_HD < half, pltpu.roll(y, width - half, 1), pltpu.roll(y, half, 1))
        return y * cos_ref[:, :width] + partner * sin_ref[:, :width]

    q_ref[0] = norm_rope(a[:, :nq], qw_ref[...], nq).astype(BF16)
    k = norm_rope(a[:, nq:nq + nk], kw_ref[...], nk)
    v = a[:, nq + nk:]
    lane = lax.broadcasted_iota(jnp.int32, k.shape, 1)
    first = lane < ATT_HD
    for x, ref in ((k, kp_ref), (v, vp_ref)):
        x0 = jnp.where(first, x, 0.0)
        x1 = jnp.where(first, 0.0, x)
        ref[0, 0] = x0.astype(BF16)
        ref[0, 1] = pltpu.roll(x0, ATT_HD, 1).astype(BF16)
        ref[0, 2] = pltpu.roll(x1, ATT_HD, 1).astype(BF16)
        ref[0, 3] = x1.astype(BF16)


def _att_prep(u_att, cosf, sinf, qw, kw, bd):
    b, l, c = u_att.shape
    tl = min(ROW_TILE, l)
    nq = ATT_HEADS * ATT_HD
    nk = ATT_KV_HEADS * ATT_HD
    full = lambda a: pl.BlockSpec(a.shape, lambda bi, i: (0,) * a.ndim)
    tab = pl.BlockSpec((tl, nq), lambda bi, i: (i, 0))
    placed = pl.BlockSpec((1, 4, tl, nk), lambda bi, i: (bi, 0, i, 0))
    return pl.pallas_call(
        _att_prep_body,
        grid=(b, l // tl),
        in_specs=[pl.BlockSpec((1, tl, c), lambda bi, i: (bi, i, 0)), tab, tab, full(qw), full(kw), full(bd)],
        out_specs=[pl.BlockSpec((1, tl, nq), lambda bi, i: (bi, i, 0)), placed, placed],
        out_shape=[jax.ShapeDtypeStruct((b, l, nq), BF16),
                   jax.ShapeDtypeStruct((b, 4, l, nk), BF16), jax.ShapeDtypeStruct((b, 4, l, nk), BF16)],
        compiler_params=_cparams("parallel", "parallel"),
        name="att_prep",
    )(u_att, cosf, sinf, qw, kw, bd)


def _attn_body(q_ref, k_ref, v_ref, o_ref):
    for m in range(ATT_HEADS // 2):
        q = q_ref[0, :, m * LANES:(m + 1) * LANES]
        kv = (2 * m) // (ATT_HEADS // ATT_KV_HEADS)
        acc = None
        for half in range(2):
            s = _nt(q, k_ref[0, 2 * kv + half])
            p = jnp.exp(s - jnp.max(s, axis=-1, keepdims=True))
            denom = jnp.sum(p, axis=-1, keepdims=True)
            o = jnp.dot(p.astype(BF16), v_ref[0, 2 * kv + half], preferred_element_type=F32) / denom
            acc = o if acc is None else acc + o
        o_ref[0, :, m * LANES:(m + 1) * LANES] = acc.astype(BF16)


def _attention(q, kp, vp):
    b, l, nq = q.shape
    s = kp.shape[2]
    tq = min(ATT_TQ, l)
    kv_spec = pl.BlockSpec((1, 4, s, kp.shape[3]), lambda bi, i: (bi, 0, 0, 0))
    return pl.pallas_call(
        _attn_body,
        grid=(b, l // tq),
        in_specs=[pl.BlockSpec((1, tq, nq), lambda bi, i: (bi, i, 0)), kv_spec, kv_spec],
        out_specs=pl.BlockSpec((1, tq, nq), lambda bi, i: (bi, i, 0)),
        out_shape=jax.ShapeDtypeStruct((b, l, nq), BF16),
        compiler_params=_cparams("parallel", "arbitrary"),
        name="attention",
    )(q, kp, vp)


def _dft_body(cl_ref, sl_ref, pq_ref, o_ref):
    p = pq_ref[0, :, :BRANCH_W]
    q = pq_ref[0, :, BRANCH_W:]
    y = (jnp.dot(cl_ref[...], p, preferred_element_type=F32)
         - jnp.dot(sl_ref[...], q, preferred_element_type=F32))
    o_ref[0] = y.astype(BF16)


def _seq_dft(pq, cl, sl):
    b, l, w2 = pq.shape
    tn = min(512, l)
    tab = pl.BlockSpec((tn, l), lambda i, bi: (i, 0))
    return pl.pallas_call(
        _dft_body,
        grid=(l // tn, b),
        in_specs=[tab, tab, pl.BlockSpec((1, l, w2), lambda i, bi: (bi, 0, 0))],
        out_specs=pl.BlockSpec((1, tn, BRANCH_W), lambda i, bi: (bi, i, 0)),
        out_shape=jax.ShapeDtypeStruct((b, l, BRANCH_W), BF16),
        compiler_params=_cparams("parallel", "parallel"),
        name="seq_dft",
    )(cl, sl, pq)


def _dft_tables(n, scale_dtype=BF16):
    k = (jnp.arange(n, dtype=jnp.int32)[:, None] * jnp.arange(n, dtype=jnp.int32)[None, :]) % n
    ang = k.astype(F32) * (2.0 * np.pi / n)
    s = n ** -0.5
    return (jnp.cos(ang) * s).astype(scale_dtype), (jnp.sin(ang) * s).astype(scale_dtype)


def _channel_dft_weight():
    c, s = _dft_tables(FN_GW, F32)
    eye = jnp.eye(FN_GROUPS, dtype=F32)
    return jnp.concatenate([jnp.kron(eye, c), jnp.kron(eye, s)], axis=1).astype(BF16)


def _merge_body(x_ref, of_ref, ob_ref, z_ref, ya_ref, yc_ref, yf_ref, mod_ref, nw_ref, wg_ref, bg_ref,
                wb_ref, wo_ref, lnw_ref, lnb_ref, rw_ref, rb_ref, x1_ref, h2_ref, lg_ref, *, alpha):
    x = x_ref[0]
    sh1, sc1, g1, sh2, sc2 = (mod_ref[0, j:j + 1, :] for j in range(5))
    h = (x * (1.0 + sc1) + sh1).astype(BF16)
    o = of_ref[0] + ob_ref[0]
    z = z_ref[0].astype(F32)
    parts = []
    for hd in range(DN_HEADS):
        cols = slice(hd * DN_DV, (hd + 1) * DN_DV)
        oh = o[:, cols]
        oh = oh * lax.rsqrt(jnp.mean(oh * oh, axis=-1, keepdims=True) + RMS_EPS) * nw_ref[...]
        parts.append(oh * _silu(z[:, cols]))
    y_dn = jnp.concatenate(parts, axis=-1).astype(BF16)
    ys = (y_dn, ya_ref[0], yc_ref[0], yf_ref[0])
    m = None
    for j in range(N_BRANCH):
        cols = slice(j * D_MODEL, (j + 1) * D_MODEL)
        gate = jax.nn.sigmoid(jnp.dot(h, wg_ref[:, cols], preferred_element_type=F32) + bg_ref[:, cols])
        term = gate * jnp.dot(ys[j], wb_ref[j], preferred_element_type=F32)
        m = term if m is None else m + term
    out = jnp.dot(m.astype(BF16), wo_ref[...], preferred_element_type=F32)
    x1 = _ln_rows(alpha * x + g1 * out, lnw_ref[...], lnb_ref[...])
    x1_ref[0] = x1
    h2 = x1 * (1.0 + sc2) + sh2
    h2_ref[0] = h2.astype(BF16)
    lg_ref[0] = _dot3(h2, rw_ref[...]) + rb_ref[...]


def _merge(x, o_f, o_b, z, y_att, y_conf, y_fn, mod, wp, alpha):
    b, l, d = x.shape
    tl = min(ROW_TILE, l)
    row = lambda w: pl.BlockSpec((1, tl, w), lambda bi, i: (bi, i, 0))
    full = lambda a: pl.BlockSpec(a.shape, lambda bi, i: (0,) * a.ndim)
    weights = (wp['dn_norm_w'], wp['w_gate'], wp['b_gate'], wp['w_branch'], wp['w_out'],
               wp['ln1_w'], wp['ln1_b'], wp['router_w'], wp['router_b'])
    return pl.pallas_call(
        functools.partial(_merge_body, alpha=alpha),
        grid=(b, l // tl),
        in_specs=[row(d), row(BRANCH_W), row(BRANCH_W), row(BRANCH_W), row(BRANCH_W), row(BRANCH_W),
                  row(BRANCH_W), pl.BlockSpec((1, 8, d), lambda bi, i: (bi, 0, 0))]
                 + [full(a) for a in weights],
        out_specs=[row(d), row(d), row(LANES)],
        out_shape=[jax.ShapeDtypeStruct((b, l, d), F32), jax.ShapeDtypeStruct((b, l, d), BF16),
                   jax.ShapeDtypeStruct((b, l, LANES), F32)],
        compiler_params=_cparams("parallel", "parallel"),
        name="merge",
    )(x, o_f, o_b, z, y_att, y_conf, y_fn, mod, *weights)


def _ln2_body(x_ref, y_ref, p_ref, g_ref, w_ref, b_ref, o_ref, *, alpha):
    f = y_ref[0] * p_ref[0]
    for k in range(1, TOP_K):
        f = f + y_ref[k] * p_ref[k]
    o_ref[0] = _ln_rows(alpha * x_ref[0] + g_ref[0] * f, w_ref[...], b_ref[...])


def _ln2(x1, y_slots, p_slots, token_offset, g2, w, bias, alpha):
    b, l, d = x1.shape
    tl = math.gcd(min(ROW_TILE, l), token_offset) if token_offset else min(ROW_TILE, l)
    per_b = l // tl
    off = token_offset // tl
    row = pl.BlockSpec((1, tl, d), lambda bi, i: (bi, i, 0))
    slots = lambda cw: pl.BlockSpec((TOP_K, tl, cw), lambda bi, i: (0, off + bi * per_b + i, 0))
    vec = pl.BlockSpec((1, 1, d), lambda bi, i: (bi, 0, 0))
    par = pl.BlockSpec((1, d), lambda bi, i: (0, 0))
    return pl.pallas_call(
        functools.partial(_ln2_body, alpha=alpha),
        grid=(b, per_b),
        in_specs=[row, slots(d), slots(1), vec, par, par],
        out_specs=row,
        out_shape=jax.ShapeDtypeStruct((b, l, d), F32),
        compiler_params=_cparams("parallel", "parallel"),
        name="ln2",
    )(x1, y_slots, p_slots, g2, w.reshape(1, d), bias.reshape(1, d))


def _moe_ffn_body(tile_ref, exp_ref, lo_ref, hi_ref, x_ref, wgu_ref, bgu_ref, wd_ref, bd_ref, o_ref,
                  wgu_bf, wd_bf):
    w = pl.program_id(0)
    tm = o_ref.shape[0]
    e = exp_ref[w]
    e_prev = exp_ref[jnp.maximum(w - 1, 0)]
    lo = lo_ref[w]
    hi = hi_ref[w]
    base = tile_ref[w] * tm

    @pl.when((w == 0) | (e != e_prev))
    def _():
        wgu_bf[...] = wgu_ref[0, 0].astype(BF16)
        wd_bf[...] = wd_ref[0, 0].astype(BF16)

    @pl.when(lo == base)
    def _():
        o_ref[...] = jnp.zeros_like(o_ref)

    @pl.when(hi > lo)
    def _():
        packed = x_ref[...]
        x_lo = lax.bitcast_convert_type(packed << 16, F32)
        x_hi = lax.bitcast_convert_type(packed & jnp.uint32(0xFFFF0000), F32)
        x = jnp.concatenate([x_lo, x_hi], axis=-1).astype(BF16)
        gu = jnp.dot(x, wgu_bf[...], preferred_element_type=F32) + bgu_ref[0, 0]
        gate = jnp.minimum(gu[:, :D_EXPERT], SWIGLU_LIMIT)
        up = jnp.clip(gu[:, D_EXPERT:], -SWIGLU_LIMIT, SWIGLU_LIMIT)
        act = (up + 1.0) * gate * jax.nn.sigmoid(SWIGLU_ALPHA * gate)
        y = jnp.dot(act.astype(BF16), wd_bf[...], preferred_element_type=F32) + bd_ref[0, 0]
        row = base + lax.broadcasted_iota(jnp.int32, (tm, 1), 0)
        o_ref[...] += jnp.where((row >= lo) & (row < hi), y, 0.0)


def _moe_ffn(x_packed, items, layer, w_gu, b_gu, w_down, b_down):
    a, half = x_packed.shape
    d = 2 * half
    tm = MOE_TILE_M
    n_items = items[0].shape[0]
    wmap = lambda w, tile, exp, lo, hi: (layer, exp[w], 0, 0)
    grid_spec = pltpu.PrefetchScalarGridSpec(
        num_scalar_prefetch=4,
        grid=(n_items,),
        in_specs=[
            pl.BlockSpec((tm, half), lambda w, tile, exp, lo, hi: (tile[w], 0)),
            pl.BlockSpec((1, 1, d, 2 * D_EXPERT), wmap),
            pl.BlockSpec((1, 1, 1, 2 * D_EXPERT), wmap),
            pl.BlockSpec((1, 1, D_EXPERT, d), wmap),
            pl.BlockSpec((1, 1, 1, d), wmap),
        ],
        out_specs=pl.BlockSpec((tm, d), lambda w, tile, exp, lo, hi: (tile[w], 0)),
        scratch_shapes=[pltpu.VMEM((d, 2 * D_EXPERT), BF16), pltpu.VMEM((D_EXPERT, d), BF16)],
    )
    depth = w_gu.shape[0]
    return pl.pallas_call(
        _moe_ffn_body,
        grid_spec=grid_spec,
        out_shape=jax.ShapeDtypeStruct((a, d), F32),
        compiler_params=pltpu.CompilerParams(dimension_semantics=("arbitrary",),
                                             vmem_limit_bytes=MOE_VMEM_LIMIT_BYTES),
        name="moe_ffn",
    )(*items, x_packed, w_gu, b_gu.reshape(depth, N_EXPERTS, 1, -1), w_down,
      b_down.reshape(depth, N_EXPERTS, 1, -1))


def _moe(t, logits, layer, w_gu, b_gu, w_down, b_down):
    n, d = t.shape
    a = n * TOP_K
    tm = MOE_TILE_M
    top_v, top_i = lax.top_k(logits, TOP_K)
    probs = jax.nn.softmax(top_v, axis=-1)

    flat_e = top_i.reshape(a).astype(jnp.int32)
    ids = jnp.arange(a, dtype=jnp.int32)
    order = jnp.sort(flat_e * a + ids) % a
    _, pos = lax.sort_key_val(order, ids)
    counts = jnp.sum((flat_e[:, None] == jnp.arange(N_EXPERTS, dtype=jnp.int32)[None, :]).astype(jnp.int32), axis=0)
    ends = jnp.cumsum(counts)

    n_tiles = -(-a // tm)
    cuts =jnp.sort(jnp.concatenate([jnp.arange(n_tiles, dtype=jnp.int32) * tm, ends[:-1].astype(jnp.int32)]))
    item_lo = cuts
    item_hi = jnp.concatenate([cuts[1:], jnp.array([a], jnp.int32)])
    item_tile = jnp.minimum(item_lo // tm, n_tiles - 1)
    item_exp = jnp.minimum(jnp.searchsorted(ends, item_lo, side='right'), N_EXPERTS - 1).astype(jnp.int32)

    half = d // 2
    packed = lax.bitcast_convert_type(jnp.stack([t[:, :half], t[:, half:]], axis=-1), jnp.uint32)
    x_sorted = jnp.take(packed, order // TOP_K, axis=0)
    y = _moe_ffn(x_sorted, (item_tile, item_exp, item_lo, item_hi), layer, w_gu, b_gu, w_down, b_down)
    pos_slots = pos.reshape(n, TOP_K).T.reshape(a)
    y_slots = jnp.take(y, pos_slots, axis=0).reshape(TOP_K, n, d)
    return y_slots, probs.T.reshape(TOP_K, n, 1)


def _head_perm():
    return np.concatenate([np.arange(0, ATT_HD, 2), np.arange(1, ATT_HD, 2)])


def _prep_layer_params(lp):
    w = lp['w_in']
    perm = _head_perm()
    qperm = (np.arange(ATT_HEADS)[:, None] * ATT_HD + perm[None, :]).reshape(-1)
    kperm = (np.arange(ATT_KV_HEADS)[:, None] * ATT_HD + perm[None, :]).reshape(-1)
    o = 0
    cols = {}
    for name, width in (('dn', 2048), ('ba', 16), ('aq', 512), ('ak', 128), ('av', 128), ('cu', 1024), ('fu', 512)):
        cols[name] = w[:, o:o + width]
        o += width
    w_cat = jnp.concatenate([
        cols['dn'], jnp.pad(cols['ba'], ((0, 0), (0, LANES - 16))),
        cols['aq'][:, qperm], cols['ak'][:, kperm], cols['av'], cols['cu'], cols['fu']], axis=1).astype(BF16)
    wp = dict(lp)
    wp['w_cat'] = w_cat
    wp['att_qw'] = (jnp.tile(lp['att_q_norm_w'][perm], ATT_HEADS) * (ATT_HD ** -0.5)).reshape(1, -1)
    wp['att_kw'] = jnp.tile(lp['att_k_norm_w'][perm], ATT_KV_HEADS).reshape(1, -1)
    wp['dn_norm_w'] = lp['dn_norm_w'].reshape(1, -1)
    wp['w_gate'] = lp['w_gate'].astype(BF16)
    wp['b_gate'] = lp['b_gate'].reshape(1, -1)
    wp['w_branch'] = lp['w_branch'].astype(BF16)
    wp['w_out'] = lp['w_out'].astype(BF16)
    wp['ln1_w'] = lp['ln1_w'].reshape(1, -1)
    wp['ln1_b'] = lp['ln1_b'].reshape(1, -1)
    wp['router_w'] = jnp.pad(lp['router_w'], ((0, 0), (0, LANES - N_EXPERTS)))
    wp['router_b'] = jnp.pad(lp['router_b'], (0, LANES - N_EXPERTS)).reshape(1, -1)
    return wp


def _rope_tables(l):
    rows = l // GRID_W
    row = jnp.repeat(jnp.arange(rows), GRID_W).astype(F32)
    col = jnp.tile(jnp.arange(GRID_W), rows).astype(F32)
    half = ATT_HD // 2
    inv = ROPE_THETA ** (-jnp.arange(0, half, 2, dtype=F32) / half)
    ang = jnp.concatenate([row[:, None] * inv, col[:, None] * inv], axis=-1)
    cos, sin = jnp.cos(ang), jnp.sin(ang)
    cosf = jnp.tile(jnp.concatenate([cos, cos], axis=-1), (1, ATT_HEADS))
    sinf = jnp.tile(jnp.concatenate([-sin, sin], axis=-1), (1, ATT_HEADS))
    return cosf, sinf


def _adaln(cvec, w, b):
    m = jax.nn.silu(cvec) @ w + b
    return m.reshape(m.shape[:-1] + (6, D_MODEL))


def _mixers(xs, mod, wp, tabs, s0f, s0b, alpha):
    cosf, sinf, bd, w_fn, cl, sl = tabs
    u_qkv, u_z, u_ba, u_att, u_conf, pq = _inproj(xs, mod[:, 1:2], mod[:, 0:1], wp['w_cat'], w_fn)
    qf, kf, vf, gb = _dn_prep(u_qkv, u_ba, wp['dn_conv_w'], wp['dn_a_log'], wp['dn_dt_bias'])
    o_f, o_b, s_f, s_b = _dn_rec(_dn_chunk(qf, kf, vf, gb), s0f, s0b)
    q_rot, kp, vp = _att_prep(u_att, cosf, sinf, wp['att_qw'], wp['att_kw'], bd)
    y_conf = _conformer(u_conf, wp['conf_dw_w'], wp['conf_dw_b'], wp['conf_ln_w'], wp['conf_ln_b'])
    y_fn = _seq_dft(pq, cl, sl)
    return dict(o_f=o_f, o_b=o_b, z=u_z, q=q_rot, kp=kp, vp=vp, y_conf=y_conf, y_fn=y_fn, s_f=s_f, s_b=s_b)


def _trunk_layer(x, cx, c, c_ctx, lp, moe_w, layer, tabs_l, tabs_c, last, alpha):
    b, l, d = x.shape
    lc = cx.shape[1]
    wp = _prep_layer_params(lp)
    mod = _adaln(c, lp['w_ada'], lp['b_ada'])
    mod_c = jnp.broadcast_to(_adaln(c_ctx, lp['w_ada'], lp['b_ada'])[None], (b, 6, d))
    pad8 = lambda m: jnp.pad(m, ((0, 0), (0, 2), (0, 0)))
    s0 = jnp.zeros((b, DN_HEADS, DN_DK, DN_DV), F32)

    mc = _mixers(cx, mod_c, wp, tabs_c, s0, s0, alpha)
    ml = _mixers(x, mod, wp, tabs_l, mc['s_f'], mc['s_b'], alpha)
    kp = jnp.concatenate([mc['kp'], ml['kp']], axis=2)
    vp = jnp.concatenate([mc['vp'], ml['vp']], axis=2)
    y_att = _attention(ml['q'], kp, vp)
    x1, h2, logits = _merge(x, ml['o_f'], ml['o_b'], ml['z'], y_att, ml['y_conf'], ml['y_fn'], pad8(mod), wp, alpha)
    if last:
        y_slots, p_slots = _moe(h2.reshape(-1, d), logits.reshape(-1, LANES)[:, :N_EXPERTS], layer, *moe_w)
        offset = 0
    else:
        yc_att = _attention(mc['q'], mc['kp'], mc['vp'])
        cx1, hc2, logits_c = _merge(cx, mc['o_f'], mc['o_b'], mc['z'], yc_att, mc['y_conf'], mc['y_fn'],
                                    pad8(mod_c), wp, alpha)
        t_all = jnp.concatenate([hc2.reshape(-1, d), h2.reshape(-1, d)], axis=0)
        lg_all = jnp.concatenate([logits_c.reshape(-1, LANES), logits.reshape(-1, LANES)], axis=0)[:, :N_EXPERTS]
        y_slots, p_slots = _moe(t_all, lg_all, layer, *moe_w)
        offset = b * lc
        cx = _ln2(cx1, y_slots, p_slots, 0, mod_c[:, 5:6], lp['ln2_w'], lp['ln2_b'], alpha)
    x = _ln2(x1, y_slots, p_slots, offset, mod[:, 5:6], lp['ln2_w'], lp['ln2_b'], alpha)
    return x, cx


def _tables(l, rope):
    nq = ATT_HEADS * ATT_HD
    if rope:
        cosf, sinf = _rope_tables(l)
    else:
        cosf, sinf = jnp.ones((l, nq), F32), jnp.zeros((l, nq), F32)
    bd = jnp.kron(jnp.eye(ATT_HEADS, dtype=F32), jnp.full((ATT_HD, ATT_HD), 1.0 / ATT_HD, F32)).astype(BF16)
    cl, sl = _dft_tables(l)
    return cosf, sinf, bd, _channel_dft_weight(), cl, sl


def kernel(x, c, ctx, c_ctx, w_ada, b_ada, w_in, dn_conv_w, dn_a_log, dn_dt_bias, dn_norm_w, att_q_norm_w, att_k_norm_w, conf_dw_w, conf_dw_b, conf_ln_w, conf_ln_b, w_branch, w_gate, b_gate, w_out, ln1_w, ln1_b, router_w, router_b, exp_w_gu, exp_b_gu, exp_w_down, exp_b_down, ln2_w, ln2_b):
    depth = w_in.shape[0]
    alpha = (2 * depth) ** 0.25
    tabs_l = _tables(x.shape[1], True)
    tabs_c = _tables(ctx.shape[1], False)
    cx = ctx
    for i in range(depth):
        lp = dict(w_ada=w_ada[i], b_ada=b_ada[i], w_in=w_in[i], dn_conv_w=dn_conv_w[i],
                  dn_a_log=dn_a_log[i], dn_dt_bias=dn_dt_bias[i], dn_norm_w=dn_norm_w[i],
                  att_q_norm_w=att_q_norm_w[i], att_k_norm_w=att_k_norm_w[i],
                  conf_dw_w=conf_dw_w[i], conf_dw_b=conf_dw_b[i], conf_ln_w=conf_ln_w[i],
                  conf_ln_b=conf_ln_b[i], w_branch=w_branch[i], w_gate=w_gate[i], b_gate=b_gate[i],
                  w_out=w_out[i], ln1_w=ln1_w[i], ln1_b=ln1_b[i], router_w=router_w[i],
                  router_b=router_b[i], ln2_w=ln2_w[i], ln2_b=ln2_b[i])
        moe_w = (exp_w_gu, exp_b_gu, exp_w_down, exp_b_down)
        x, cx = _trunk_layer(x, cx, c, c_ctx, lp, moe_w, i, tabs_l, tabs_c, i == depth - 1, alpha)
    return x
```

```python
import functools
import math

import jax
import jax.numpy as jnp
import numpy as np
from jax import lax
from jax.experimental import pallas as pl
from jax.experimental.pallas import tpu as pltpu

F32 = jnp.float32
BF16 = jnp.bfloat16

D_MODEL = 1024
GRID_W = 64
BRANCH_W = D_MODEL // 2
N_BRANCH = 4
DN_DK = 128
DN_DV = 128
DN_HEADS = BRANCH_W // DN_DV
DN_CONV = 5
DN_CHUNK = 64
ATT_HD = 64
ATT_HEADS = BRANCH_W // ATT_HD
ATT_KV_HEADS = ATT_HEADS // 4
ROPE_THETA = 10000.0
CONF_W = BRANCH_W
CONF_K = 31
FN_GROUPS = 4
FN_GW = BRANCH_W // FN_GROUPS
N_EXPERTS = 32
TOP_K = 4
D_EXPERT = D_MODEL
SWIGLU_LIMIT = 7.0
SWIGLU_ALPHA = 1.702
LN_EPS = 1e-6
RMS_EPS = 1e-6

LANES = 128
SUBLANES = 8
HALO = 16
VMEM_LIMIT_BYTES = 52 * 1024 * 1024
MOE_VMEM_LIMIT_BYTES = 58 * 1024 * 1024
MOE_TILE_M = 512
ROW_TILE = 256
MATMUL_ROW_TILE = 512
ATT_TQ = 256
CONV_ROWS = 32
DN_CHUNKS_PER_STEP = 2

C_QKV = (0, 1536)
C_Z = (1536, 2048)
C_BA = (2048, 2176)
C_ATT = (2176, 2944)
C_CONF = (2944, 3968)
C_FN = (3968, 4480)
W_IN_COLS = 4480


def _cparams(*sem):
    return pltpu.CompilerParams(dimension_semantics=sem, vmem_limit_bytes=VMEM_LIMIT_BYTES)


def _nt(a, b):
    return lax.dot_general(a, b, (((1,), (1,)), ((), ())), preferred_element_type=F32)


def _split3(x):
    hi = x.astype(BF16)
    r1 = x - hi.astype(F32)
    mid = r1.astype(BF16)
    lo = (r1 - mid.astype(F32)).astype(BF16)
    return hi, mid, lo


def _dot_exact_lhs(e, x):
    hi, mid, lo = _split3(x)
    return (jnp.dot(e, hi, preferred_element_type=F32) + jnp.dot(e, mid, preferred_element_type=F32)
            + jnp.dot(e, lo, preferred_element_type=F32))


def _dot3(a, b):
    a_hi = a.astype(BF16)
    a_lo = (a - a_hi.astype(F32)).astype(BF16)
    b_hi = b.astype(BF16)
    b_lo = (b - b_hi.astype(F32)).astype(BF16)
    return (jnp.dot(a_hi, b_hi, preferred_element_type=F32) + jnp.dot(a_hi, b_lo, preferred_element_type=F32)
            + jnp.dot(a_lo, b_hi, preferred_element_type=F32))


def _stacked_dot3(lhs_list, b):
    rows = lhs_list[0].shape[0]
    his = [a.astype(BF16) for a in lhs_list]
    los = [(a - h.astype(F32)).astype(BF16) for a, h in zip(lhs_list, his)]
    b_hi = b.astype(BF16)
    b_lo = (b - b_hi.astype(F32)).astype(BF16)
    m = len(lhs_list)
    r1 = jnp.dot(jnp.concatenate(his + los, axis=0), b_hi, preferred_element_type=F32)
    r2 = jnp.dot(jnp.concatenate(his, axis=0), b_lo, preferred_element_type=F32)
    blk = lambda r, j: r[j * rows:(j + 1) * rows]
    return [blk(r1, j) + blk(r1, m + j) + blk(r2, j) for j in range(m)]


def _silu(x):
    return x * jax.nn.sigmoid(x)


def _ln_rows(x, w, b):
    mu = jnp.mean(x, axis=-1, keepdims=True)
    xc = x - mu
    var = jnp.mean(xc * xc, axis=-1, keepdims=True)
    return xc * lax.rsqrt(var + LN_EPS) * w + b


def _inproj_body(x_ref, sc_ref, sh_ref, w_ref, wfn_ref, qkv_ref, z_ref, ba_ref, att_ref, conf_ref, pq_ref):
    h = (x_ref[0] * (1.0 + sc_ref[0]) + sh_ref[0]).astype(BF16)

    def proj(c):
        return jnp.dot(h, w_ref[:, c[0]:c[1]], preferred_element_type=F32)

    qkv_ref[0] = proj(C_QKV).astype(BF16)
    z_ref[0] = proj(C_Z).astype(BF16)
    ba_ref[0] = proj(C_BA)
    att_ref[0] = proj(C_ATT).astype(BF16)
    conf_ref[0] = proj(C_CONF).astype(BF16)
    fu = proj(C_FN).astype(BF16)
    pq_ref[0] = jnp.dot(fu, wfn_ref[...], preferred_element_type=F32).astype(BF16)


def _resident(a):
    return pl.BlockSpec(a.shape, lambda bi, i: (0,) * a.ndim, pipeline_mode=pl.Buffered(1))


def _inproj(x, sc, sh, w_cat, w_fn):
    b, l, d = x.shape
    tl = min(MATMUL_ROW_TILE, l)
    widths = [c[1] - c[0] for c in (C_QKV, C_Z, C_BA, C_ATT, C_CONF)] + [2 * BRANCH_W]
    dtypes = [BF16, BF16, F32, BF16, BF16, BF16]
    row = lambda w: pl.BlockSpec((1, tl, w), lambda bi, i: (bi, i, 0))
    vec = pl.BlockSpec((1, 1, d), lambda bi, i: (bi, 0, 0))
    return pl.pallas_call(
        _inproj_body,
        grid=(b, l // tl),
        in_specs=[row(d), vec, vec, _resident(w_cat), _resident(w_fn)],
        out_specs=[row(w) for w in widths],
        out_shape=[jax.ShapeDtypeStruct((b, l, w), dt) for w, dt in zip(widths, dtypes)],
        compiler_params=_cparams("parallel", "parallel"),
        name="inproj",
    )(x, sc, sh, w_cat, w_fn)


def _halo_specs(tl, l, c):
    per = tl // HALO
    last = l // HALO - 1
    prev = pl.BlockSpec((1, HALO, c), lambda bi, i: (bi, jnp.maximum(i * per - 1, 0), 0))
    cur = pl.BlockSpec((1, tl, c), lambda bi, i: (bi, i, 0))
    nxt = pl.BlockSpec((1, HALO, c), lambda bi, i: (bi, jnp.minimum((i + 1) * per, last), 0))
    return prev, cur, nxt


def _tap_shifts(taps):
    pad = (taps - 1) // 2
    return sorted({(HALO + k - pad) % SUBLANES for k in range(taps)})


def _fill_halo_buffer(buf, prev, cur, nxt, tl, taps):
    i = pl.program_id(1)
    n = pl.num_programs(1)
    rows = tl + 2 * HALO
    buf[0, HALO:HALO + tl] = cur
    buf[0, 0:HALO] = jnp.where(i > 0, prev, 0.0)
    buf[0, HALO + tl:rows] = jnp.where(i < n - 1, nxt, 0.0)
    for s in _tap_shifts(taps):
        if s:
            buf[s, 0:rows - SUBLANES] = buf[0, s:s + rows - SUBLANES]


def _depthwise_rows(buf, w_ref, r0, taps, c0, c1):
    pad = (taps - 1) // 2
    acc = None
    for k in range(taps):
        start = HALO + r0 + k - pad
        s = start % SUBLANES
        term = buf[s, start - s:start - s + CONV_ROWS, c0:c1] * w_ref[k:k + 1, c0:c1]
        acc = term if acc is None else acc + term
    return acc


def _conf_body(up_ref, u_ref, un_ref, w_ref, b_ref, lnw_ref, lnb_ref, o_ref, buf):
    tl = u_ref.shape[1]

    def glu(u):
        u = u.astype(F32)
        return u[:, :CONF_W] * jax.nn.sigmoid(u[:, CONF_W:])

    _fill_halo_buffer(buf, glu(up_ref[0]), glu(u_ref[0]), glu(un_ref[0]), tl, CONF_K)
    for r0 in range(0, tl, CONV_ROWS):
        y = _depthwise_rows(buf, w_ref, r0, CONF_K, 0, CONF_W) + b_ref[...]
        o_ref[0, r0:r0 + CONV_ROWS, :] = _silu(_ln_rows(y, lnw_ref[...], lnb_ref[...])).astype(BF16)


def _conformer(u, dw_w, dw_b, ln_w, ln_b):
    b, l, c = u.shape
    tl = min(ROW_TILE, l)
    prev, cur, nxt = _halo_specs(tl, l, c)
    full = lambda a: pl.BlockSpec(a.shape, lambda bi, i: (0,) * a.ndim)
    args = (dw_w, dw_b.reshape(1, -1), ln_w.reshape(1, -1), ln_b.reshape(1, -1))
    return pl.pallas_call(
        _conf_body,
        grid=(b, l // tl),
        in_specs=[prev, cur, nxt] + [full(a) for a in args],
        out_specs=pl.BlockSpec((1, tl, CONF_W), lambda bi, i: (bi, i, 0)),
        out_shape=jax.ShapeDtypeStruct((b, l, CONF_W), BF16),
        scratch_shapes=[pltpu.VMEM((SUBLANES, tl + 2 * HALO, CONF_W), F32)],
        compiler_params=_cparams("parallel", "parallel"),
        name="conformer",
    )(u, u, u, *args)


def _dn_prep_body(up_ref, u_ref, un_ref, ba_ref, w_ref, alog_ref, dtb_ref, q_ref, k_ref, v_ref, gb_ref, buf):
    tl = u_ref.shape[1]
    hk = DN_HEADS * DN_DK
    _fill_halo_buffer(buf, up_ref[0].astype(F32), u_ref[0].astype(F32), un_ref[0].astype(F32), tl, DN_CONV)
    for r0 in range(0, tl, CONV_ROWS):
        rows = slice(r0, r0 + CONV_ROWS)
        for hd in range(DN_HEADS):
            for part, ref in ((0, q_ref), (1, k_ref), (2, v_ref)):
                c0 = part * hk + hd * DN_DK
                y = _silu(_depthwise_rows(buf, w_ref, r0, DN_CONV, c0, c0 + DN_DK))
                if part < 2:
                    y = y * lax.rsqrt(jnp.sum(y * y, axis=-1, keepdims=True) + 1e-6)
                if part == 0:
                    y = y * (DN_DK ** -0.5)
                ref[0, rows, hd * DN_DK:(hd + 1) * DN_DK] = y
    ba = ba_ref[0]
    t = ba + dtb_ref[...]
    softplus = jnp.maximum(t, 0.0) + jnp.log1p(jnp.exp(-jnp.abs(t)))
    g = -jnp.exp(alog_ref[...]) * softplus
    lane = lax.broadcasted_iota(jnp.int32, ba.shape, 1)
    gb_ref[0] = jnp.where(lane < 2 * DN_HEADS, jax.nn.sigmoid(ba), g)


def _dn_prep(u_qkv, u_ba, conv_w, a_log, dt_bias):
    b, l, c = u_qkv.shape
    tl = min(ROW_TILE, l)
    prev, cur, nxt = _halo_specs(tl, l, c)
    nh = 2 * DN_HEADS
    alog_row = jnp.zeros((1, LANES), F32).at[0, nh:2 * nh].set(a_log.reshape(-1))
    dtb_row = jnp.zeros((1, LANES), F32).at[0, nh:2 * nh].set(dt_bias.reshape(-1))
    row = lambda w: pl.BlockSpec((1, tl, w), lambda bi, i: (bi, i, 0))
    full = lambda a: pl.BlockSpec(a.shape, lambda bi, i: (0,) * a.ndim)
    return pl.pallas_call(
        _dn_prep_body,
        grid=(b, l // tl),
        in_specs=[prev, cur, nxt, row(LANES), full(conv_w), full(alog_row), full(dtb_row)],
        out_specs=[row(BRANCH_W), row(BRANCH_W), row(BRANCH_W), row(LANES)],
        out_shape=[jax.ShapeDtypeStruct((b, l, BRANCH_W), F32)] * 3 + [jax.ShapeDtypeStruct((b, l, LANES), F32)],
        scratch_shapes=[pltpu.VMEM((SUBLANES, tl + 2 * HALO, c), F32)],
        compiler_params=_cparams("parallel", "parallel"),
        name="dn_prep",
    )(u_qkv, u_qkv, u_qkv, u_ba, conv_w, alog_row, dtb_row)


def _dn_chunk_body(q_ref, k_ref, v_ref, gb_ref, uf_ref, wqf_ref, qktf_ref, glf_ref,
                   ub_ref, wqb_ref, qktb_ref, glb_ref):
    C = DN_CHUNK
    nh = 2 * DN_HEADS
    ii = lax.broadcasted_iota(jnp.int32, (C, C), 0)
    jj = lax.broadcasted_iota(jnp.int32, (C, C), 1)
    eye = (ii == jj).astype(F32)
    eye_l = (lax.broadcasted_iota(jnp.int32, (LANES, LANES), 0)
             == lax.broadcasted_iota(jnp.int32, (LANES, LANES), 1)).astype(BF16)
    lower = ((ii >= jj), (ii <= jj))
    strict = ((ii > jj), (ii < jj))
    outs = ((uf_ref, wqf_ref, qktf_ref, glf_ref), (ub_ref, wqb_ref, qktb_ref, glb_ref))
    n_chunks = q_ref.shape[1] // C

    cum = {}
    for ck in range(n_chunks):
        gb = gb_ref[0, ck * C:(ck + 1) * C, :]
        for d in range(2):
            gcum = _dot_exact_lhs(lower[d].astype(BF16), gb)
            hi, mid, lo = _split3(gcum)
            cum[ck, d] = (gb, gcum, _nt(eye_l, hi) + _nt(eye_l, mid) + _nt(eye_l, lo))
            last = C - 1 if d == 0 else 0
            outs[d][3][0, ck * 8:(ck + 1) * 8, :] = jnp.broadcast_to(gcum[last:last + 1, :], (8, LANES))

    units = [(ck, d, hd) for ck in range(n_chunks) for d in range(2) for hd in range(DN_HEADS)]
    kk, kbs, decay, n = {}, {}, {}, {}
    for ck in range(n_chunks):
        for hd in range(DN_HEADS):
            kb = k_ref[0, ck * C:(ck + 1) * C, hd * DN_DK:(hd + 1) * DN_DK].astype(BF16)
            kbs[ck, hd] = kb
            kk[ck, hd] = _nt(kb, kb)
    for (ck, d, hd) in units:
        gb, gcum, gcum_t = cum[ck, d]
        cg = nh + d * DN_HEADS + hd
        cb = d * DN_HEADS + hd
        diff = gcum[:, cg:cg + 1] - gcum_t[cg:cg + 1, :]
        dec = jnp.where(lower[d], jnp.exp(jnp.where(lower[d], diff, 0.0)), 0.0)
        decay[ck, d, hd] = dec
        n[ck, d, hd] = jnp.where(strict[d], kk[ck, hd] * dec * gb[:, cb:cb + 1], 0.0)

    t = {u: eye - n[u] for u in units}
    p = {u: _stacked_dot3([n[u]], n[u])[0] for u in units}
    for _ in range(4):
        tp = {u: _stacked_dot3([t[u], p[u]], p[u]) for u in units}
        t = {u: t[u] + tp[u][0] for u in units}
        p = {u: tp[u][1] for u in units}
    t = {u: t[u] + _stacked_dot3([t[u]], p[u])[0] for u in units}

    for (ck, d, hd) in units:
        gb, gcum, _ = cum[ck, d]
        u_ref, wq_ref, qkt_ref, _ = outs[d]
        cg = nh + d * DN_HEADS + hd
        cb = d * DN_HEADS + hd
        rows = slice(ck * C, (ck + 1) * C)
        cols = slice(hd * DN_DK, (hd + 1) * DN_DK)
        q = q_ref[0, rows, cols]
        k = k_ref[0, rows, cols]
        v = v_ref[0, rows, cols]
        beta = gb[:, cb:cb + 1]
        gc = gcum[:, cg:cg + 1]
        last = C - 1 if d == 0 else 0
        eg = jnp.exp(gc)
        rhs = jnp.concatenate([v * beta, k * (beta * eg)], axis=-1).astype(BF16)
        sol = jnp.dot(t[ck, d, hd].astype(BF16), rhs, preferred_element_type=F32)
        u_ref[0, hd, rows, :] = sol[:, :DN_DV]
        wq_ref[0, hd, ck * 2 * C:ck * 2 * C + C, :] = sol[:, DN_DV:].astype(BF16)
        wq_ref[0, hd, ck * 2 * C + C:(ck + 1) * 2 * C, :] = (q * eg).astype(BF16)
        qk = _nt(q.astype(BF16), kbs[ck, hd]) * decay[ck, d, hd]
        k_dec = (k * jnp.exp(gc[last:last + 1, :] - gc)).astype(BF16)
        qkt_ref[0, hd, ck * 3 * C:ck * 3 * C + C, :] = qk.astype(BF16)
        qkt_ref[0, hd, ck * 3 * C + C:(ck + 1) * 3 * C, :] = _nt(eye_l, k_dec).astype(BF16)


def _dn_chunk(qf, kf, vf, gb):
    b, l, w = qf.shape
    C = DN_CHUNK
    rows = min(DN_CHUNKS_PER_STEP * C, l)
    per = rows // C
    nc = l // C
    row = lambda cw: pl.BlockSpec((1, rows, cw), lambda bi, i: (bi, i, 0))
    head = lambda r, cw: pl.BlockSpec((1, DN_HEADS, r, cw), lambda bi, i: (bi, 0, i, 0))
    one_dir_specs = [head(rows, DN_DV), head(2 * rows, DN_DK), head(3 * rows, C),
                     pl.BlockSpec((1, per * 8, LANES), lambda bi, i: (bi, i, 0))]
    one_dir_shapes = [jax.ShapeDtypeStruct((b, DN_HEADS, l, DN_DV), F32),
                      jax.ShapeDtypeStruct((b, DN_HEADS, 2 * l, DN_DK), BF16),
                      jax.ShapeDtypeStruct((b, DN_HEADS, 3 * l, C), BF16),
                      jax.ShapeDtypeStruct((b, nc * 8, LANES), F32)]
    return pl.pallas_call(
        _dn_chunk_body,
        grid=(b, l // rows),
        in_specs=[row(w), row(w), row(w), row(LANES)],
        out_specs=one_dir_specs * 2,
        out_shape=one_dir_shapes * 2,
        compiler_params=_cparams("parallel", "parallel"),
        name="dn_chunk",
    )(qf, kf, vf, gb)


def _dn_rec_body(uf_ref, wqf_ref, qktf_ref, glf_ref, ub_ref, wqb_ref, qktb_ref, glb_ref, s0f_ref, s0b_ref,
                 of_ref, ob_ref, sf_ref, sb_ref, st_f, st_b):
    c = pl.program_id(1)
    nc = pl.num_programs(1)
    C = DN_CHUNK
    nh = 2 * DN_HEADS

    @pl.when(c == 0)
    def _():
        st_f[...] = s0f_ref[0]
        st_b[...] = s0b_ref[0]

    dirs = ((uf_ref, wqf_ref, qktf_ref, glf_ref, of_ref, st_f), (ub_ref, wqb_ref, qktb_ref, glb_ref, ob_ref, st_b))
    units = [(d, hd) for d in range(2) for hd in range(DN_HEADS)]
    s = {(d, hd): dirs[d][5][hd] for (d, hd) in units}
    r1 = {(d, hd): jnp.dot(dirs[d][1][0, hd], s[d, hd].astype(BF16), preferred_element_type=F32)
          for (d, hd) in units}
    vb = {(d, hd): (dirs[d][0][0, hd] - r1[d, hd][:C]).astype(BF16) for (d, hd) in units}
    r2 = {(d, hd): jnp.dot(dirs[d][2][0, hd], vb[d, hd], preferred_element_type=F32) for (d, hd) in units}
    for (d, hd) in units:
        cg = nh + d * DN_HEADS + hd
        decay_all = jnp.exp(dirs[d][3][0, 0:1, cg:cg + 1])
        dirs[d][4][0, :, hd * DN_DV:(hd + 1) * DN_DV] = r1[d, hd][C:] + r2[d, hd][:C]
        dirs[d][5][hd] = s[d, hd] * decay_all + r2[d, hd][C:]

    @pl.when(c == nc - 1)
    def _():
        sf_ref[0] = st_f[...]
        sb_ref[0] = st_b[...]


def _dn_rec(chunk_outs, s0f, s0b):
    uf = chunk_outs[0]
    b, _, l, _ = uf.shape
    C = DN_CHUNK
    nc = l // C
    fwd = lambda c: c
    bwd = lambda c: nc - 1 - c

    def dir_specs(cidx):
        head = lambda r, cw: pl.BlockSpec((1, DN_HEADS, r, cw), lambda bi, c: (bi, 0, cidx(c), 0))
        return [head(C, DN_DV), head(2 * C, DN_DK), head(3 * C, C),
                pl.BlockSpec((1, 8, LANES), lambda bi, c: (bi, cidx(c), 0))]

    st = pl.BlockSpec((1, DN_HEADS, DN_DK, DN_DV), lambda bi, c: (bi, 0, 0, 0))
    st_shape = jax.ShapeDtypeStruct((b, DN_HEADS, DN_DK, DN_DV), F32)
    out = lambda cidx: pl.BlockSpec((1, C, BRANCH_W), lambda bi, c: (bi, cidx(c), 0))
    return pl.pallas_call(
        _dn_rec_body,
        grid=(b, nc),
        in_specs=dir_specs(fwd) + dir_specs(bwd) + [st, st],
        out_specs=[out(fwd), out(bwd), st, st],
        out_shape=[jax.ShapeDtypeStruct((b, l, BRANCH_W), F32)] * 2 + [st_shape, st_shape],
        scratch_shapes=[pltpu.VMEM((DN_HEADS, DN_DK, DN_DV), F32)] * 2,
        compiler_params=_cparams("parallel", "arbitrary"),
        name="dn_rec",
    )(*chunk_outs, s0f, s0b)


def _att_prep_body(a_ref, cos_ref, sin_ref, qw_ref, kw_ref, bd_ref, q_ref, kp_ref, vp_ref):
    nq = ATT_HEADS * ATT_HD
    nk = ATT_KV_HEADS * ATT_HD
    a = a_ref[0].astype(F32)

    def norm_rope(x, w, width):
        ms = jnp.dot((x * x).astype(BF16), bd_ref[:width, :width], preferred_element_type=F32)
        y = x * lax.rsqrt(ms + RMS_EPS) * w
        lane = lax.broadcasted_iota(jnp.int32, y.shape, 1)
        half = ATT_HD // 2
        partner = jnp.where(lane % ATT_HD < half, pltpu.roll(y, width - half, 1), pltpu.roll(y, half, 1))
        return y * cos_ref[:, :width] + partner * sin_ref[:, :width]

    q_ref[0] = norm_rope(a[:, :nq], qw_ref[...], nq).astype(BF16)
    k = norm_rope(a[:, nq:nq + nk], kw_ref[...], nk)
    v = a[:, nq + nk:]
    lane = lax.broadcasted_iota(jnp.int32, k.shape, 1)
    first = lane < ATT_HD
    for x, ref in ((k, kp_ref), (v, vp_ref)):
        x0 = jnp.where(first, x, 0.0)
        x1 = jnp.where(first, 0.0, x)
        ref[0, 0] = x0.astype(BF16)
        ref[0, 1] = pltpu.roll(x0, ATT_HD, 1).astype(BF16)
        ref[0, 2] = pltpu.roll(x1, ATT_HD, 1).astype(BF16)
        ref[0, 3] = x1.astype(BF16)


def _att_prep(u_att, cosf, sinf, qw, kw, bd):
    b, l, c = u_att.shape
    tl = min(ROW_TILE, l)
    nq = ATT_HEADS * ATT_HD
    nk = ATT_KV_HEADS * ATT_HD
    full = lambda a: pl.BlockSpec(a.shape, lambda bi, i: (0,) * a.ndim)
    tab = pl.BlockSpec((tl, nq), lambda bi, i: (i, 0))
    placed = pl.BlockSpec((1, 4, tl, nk), lambda bi, i: (bi, 0, i, 0))
    return pl.pallas_call(
        _att_prep_body,
        grid=(b, l // tl),
        in_specs=[pl.BlockSpec((1, tl, c), lambda bi, i: (bi, i, 0)), tab, tab, full(qw), full(kw), full(bd)],
        out_specs=[pl.BlockSpec((1, tl, nq), lambda bi, i: (bi, i, 0)), placed, placed],
        out_shape=[jax.ShapeDtypeStruct((b, l, nq), BF16),
                   jax.ShapeDtypeStruct((b, 4, l, nk), BF16), jax.ShapeDtypeStruct((b, 4, l, nk), BF16)],
        compiler_params=_cparams("parallel", "parallel"),
        name="att_prep",
    )(u_att, cosf, sinf, qw, kw, bd)


def _attn_body(q_ref, k_ref, v_ref, o_ref):
    for m in range(ATT_HEADS // 2):
        q = q_ref[0, :, m * LANES:(m + 1) * LANES]
        kv = (2 * m) // (ATT_HEADS // ATT_KV_HEADS)
        acc = None
        for half in range(2):
            s = _nt(q, k_ref[0, 2 * kv + half])
            p = jnp.exp(s - jnp.max(s, axis=-1, keepdims=True))
            denom = jnp.sum(p, axis=-1, keepdims=True)
            o = jnp.dot(p.astype(BF16), v_ref[0, 2 * kv + half], preferred_element_type=F32) / denom
            acc = o if acc is None else acc + o
        o_ref[0, :, m * LANES:(m + 1) * LANES] = acc.astype(BF16)


def _attention(q, kp, vp):
    b, l, nq = q.shape
    s = kp.shape[2]
    tq = min(ATT_TQ, l)
    kv_spec = pl.BlockSpec((1, 4, s, kp.shape[3]), lambda bi, i: (bi, 0, 0, 0))
    return pl.pallas_call(
        _attn_body,
        grid=(b, l // tq),
        in_specs=[pl.BlockSpec((1, tq, nq), lambda bi, i: (bi, i, 0)), kv_spec, kv_spec],
        out_specs=pl.BlockSpec((1, tq, nq), lambda bi, i: (bi, i, 0)),
        out_shape=jax.ShapeDtypeStruct((b, l, nq), BF16),
        compiler_params=_cparams("parallel", "arbitrary"),
        name="attention",
    )(q, kp, vp)


def _dft_body(ca_ref, sa_ref, cb_ref, sb_ref, pq_ref, o_ref, ctab, stab):
    @pl.when(pl.program_id(1) == 0)
    def _():
        cb = cb_ref[...]
        sb = sb_ref[...]
        for j in range(ca_ref.shape[0]):
            ca = ca_ref[j:j + 1, :]
            sa = sa_ref[j:j + 1, :]
            rows = slice(j * GRID_W, (j + 1) * GRID_W)
            ctab[rows, :] = (ca * cb - sa * sb).astype(BF16)
            stab[rows, :] = (sa * cb + ca * sb).astype(BF16)

    p = pq_ref[0, :, :BRANCH_W]
    q = pq_ref[0, :, BRANCH_W:]
    y = (jnp.dot(ctab[...], p, preferred_element_type=F32)
         - jnp.dot(stab[...], q, preferred_element_type=F32))
    o_ref[0] = y.astype(BF16)


def _seq_dft(pq, tabs):
    ca, sa, cb, sb = tabs
    b, l, w2 = pq.shape
    tn = min(512, l)
    coarse = pl.BlockSpec((tn // GRID_W, l), lambda i, bi: (i, 0))
    fine = pl.BlockSpec((GRID_W, l), lambda i, bi: (0, 0))
    return pl.pallas_call(
        _dft_body,
        grid=(l // tn, b),
        in_specs=[coarse, coarse, fine, fine, pl.BlockSpec((1, l, w2), lambda i, bi: (bi, 0, 0))],
        out_specs=pl.BlockSpec((1, tn, BRANCH_W), lambda i, bi: (bi, i, 0)),
        out_shape=jax.ShapeDtypeStruct((b, l, BRANCH_W), BF16),
        scratch_shapes=[pltpu.VMEM((tn, l), BF16)] * 2,
        compiler_params=_cparams("parallel", "arbitrary"),
        name="seq_dft",
    )(ca, sa, cb, sb, pq)


def _seq_dft_tables(l):
    m = jnp.arange(l, dtype=jnp.int32)[None, :]

    def cs(rows, step, scale):
        k = (jnp.arange(rows, dtype=jnp.int32)[:, None] * step * m) % l
        ang = k.astype(F32) * (2.0 * np.pi / l)
        return jnp.cos(ang) * scale, jnp.sin(ang) * scale

    ca, sa = cs(l // GRID_W, GRID_W, l ** -0.5)
    cb, sb = cs(GRID_W, 1, 1.0)
    return ca, sa, cb, sb


def _dft_tables(n, scale_dtype=BF16):
    k = (jnp.arange(n, dtype=jnp.int32)[:, None] * jnp.arange(n, dtype=jnp.int32)[None, :]) % n
    ang = k.astype(F32) * (2.0 * np.pi / n)
    s = n ** -0.5
    return (jnp.cos(ang) * s).astype(scale_dtype), (jnp.sin(ang) * s).astype(scale_dtype)


def _channel_dft_weight():
    c, s = _dft_tables(FN_GW, F32)
    eye = jnp.eye(FN_GROUPS, dtype=F32)
    return jnp.concatenate([jnp.kron(eye, c), jnp.kron(eye, s)], axis=1).astype(BF16)


def _merge_body(x_ref, of_ref, ob_ref, z_ref, ya_ref, yc_ref, yf_ref, mod_ref, nw_ref, wg_ref, bg_ref,
                wb_ref, wo_ref, lnw_ref, lnb_ref, rw_ref, rb_ref, x1_ref, h2_ref, lg_ref, *, alpha):
    x = x_ref[0]
    sh1, sc1, g1, sh2, sc2 = (mod_ref[0, j:j + 1, :] for j in range(5))
    h = (x * (1.0 + sc1) + sh1).astype(BF16)
    o = of_ref[0] + ob_ref[0]
    z = z_ref[0].astype(F32)
    parts = []
    for hd in range(DN_HEADS):
        cols = slice(hd * DN_DV, (hd + 1) * DN_DV)
        oh = o[:, cols]
        oh = oh * lax.rsqrt(jnp.mean(oh * oh, axis=-1, keepdims=True) + RMS_EPS) * nw_ref[...]
        parts.append(oh * _silu(z[:, cols]))
    y_dn = jnp.concatenate(parts, axis=-1).astype(BF16)
    ys = (y_dn, ya_ref[0], yc_ref[0], yf_ref[0])
    m = None
    for j in range(N_BRANCH):
        cols = slice(j * D_MODEL, (j + 1) * D_MODEL)
        gate = jax.nn.sigmoid(jnp.dot(h, wg_ref[:, cols], preferred_element_type=F32) + bg_ref[:, cols])
        term = gate * jnp.dot(ys[j], wb_ref[j], preferred_element_type=F32)
        m = term if m is None else m + term
    out = jnp.dot(m.astype(BF16), wo_ref[...], preferred_element_type=F32)
    x1 = _ln_rows(alpha * x + g1 * out, lnw_ref[...], lnb_ref[...])
    x1_ref[0] = x1
    h2 = x1 * (1.0 + sc2) + sh2
    h2_ref[0] = h2.astype(BF16)
    lg_ref[0] = _dot3(h2, rw_ref[...]) + rb_ref[...]


def _merge(x, o_f, o_b, z, y_att, y_conf, y_fn, mod, wp, alpha):
    b, l, d = x.shape
    tl = min(MATMUL_ROW_TILE, l)
    row = lambda w: pl.BlockSpec((1, tl, w), lambda bi, i: (bi, i, 0))
    weights = (wp['dn_norm_w'], wp['w_gate'], wp['b_gate'], wp['w_branch'], wp['w_out'],
               wp['ln1_w'], wp['ln1_b'], wp['router_w'], wp['router_b'])
    return pl.pallas_call(
        functools.partial(_merge_body, alpha=alpha),
        grid=(b, l // tl),
        in_specs=[row(d), row(BRANCH_W), row(BRANCH_W), row(BRANCH_W), row(BRANCH_W), row(BRANCH_W),
                  row(BRANCH_W), pl.BlockSpec((1, 8, d), lambda bi, i: (bi, 0, 0))]
                 + [_resident(a) for a in weights],
        out_specs=[row(d), row(d), row(LANES)],
        out_shape=[jax.ShapeDtypeStruct((b, l, d), F32), jax.ShapeDtypeStruct((b, l, d), BF16),
                   jax.ShapeDtypeStruct((b, l, LANES), F32)],
        compiler_params=_cparams("parallel", "parallel"),
        name="merge",
    )(x, o_f, o_b, z, y_att, y_conf, y_fn, mod, *weights)


def _ln2_body(x_ref, y_ref, p_ref, g_ref, w_ref, b_ref, o_ref, *, alpha):
    f = y_ref[0] * p_ref[0]
    for k in range(1, TOP_K):
        f = f + y_ref[k] * p_ref[k]
    o_ref[0] = _ln_rows(alpha * x_ref[0] + g_ref[0] * f, w_ref[...], b_ref[...])


def _ln2(x1, y_slots, p_slots, token_offset, g2, w, bias, alpha):
    b, l, d = x1.shape
    tl = math.gcd(min(ROW_TILE, l), token_offset) if token_offset else min(ROW_TILE, l)
    per_b = l // tl
    off = token_offset // tl
    row = pl.BlockSpec((1, tl, d), lambda bi, i: (bi, i, 0))
    slots = lambda cw: pl.BlockSpec((TOP_K, tl, cw), lambda bi, i: (0, off + bi * per_b + i, 0))
    vec = pl.BlockSpec((1, 1, d), lambda bi, i: (bi, 0, 0))
    par = pl.BlockSpec((1, d), lambda bi, i: (0, 0))
    return pl.pallas_call(
        functools.partial(_ln2_body, alpha=alpha),
        grid=(b, per_b),
        in_specs=[row, slots(d), slots(1), vec, par, par],
        out_specs=row,
        out_shape=jax.ShapeDtypeStruct((b, l, d), F32),
        compiler_params=_cparams("parallel", "parallel"),
        name="ln2",
    )(x1, y_slots, p_slots, g2, w.reshape(1, d), bias.reshape(1, d))


def _moe_ffn_body(tile_ref, exp_ref, lo_ref, hi_ref, x_ref, wgu_ref, bgu_ref, wd_ref, bd_ref, o_ref,
                  wgu_bf, wd_bf):
    w = pl.program_id(0)
    tm = o_ref.shape[0]
    e = exp_ref[w]
    e_prev = exp_ref[jnp.maximum(w - 1, 0)]
    lo = lo_ref[w]
    hi = hi_ref[w]
    base = tile_ref[w] * tm

    @pl.when((w == 0) | (e != e_prev))
    def _():
        wgu_bf[...] = wgu_ref[0, 0].astype(BF16)
        wd_bf[...] = wd_ref[0, 0].astype(BF16)

    whole = (lo == base) & (hi == base + tm)

    @pl.when((lo == base) & jnp.logical_not(whole))
    def _():
        o_ref[...] = jnp.zeros_like(o_ref)

    @pl.when(hi > lo)
    def _():
        packed = x_ref[...]
        x_lo = lax.bitcast_convert_type(packed << 16, F32)
        x_hi = lax.bitcast_convert_type(packed & jnp.uint32(0xFFFF0000), F32)
        x = jnp.concatenate([x_lo, x_hi], axis=-1).astype(BF16)
        gu = jnp.dot(x, wgu_bf[...], preferred_element_type=F32) + bgu_ref[0, 0]
        gate = jnp.minimum(gu[:, :D_EXPERT], SWIGLU_LIMIT)
        up = jnp.clip(gu[:, D_EXPERT:], -SWIGLU_LIMIT, SWIGLU_LIMIT)
        act = (up + 1.0) * gate * jax.nn.sigmoid(SWIGLU_ALPHA * gate)
        y = jnp.dot(act.astype(BF16), wd_bf[...], preferred_element_type=F32) + bd_ref[0, 0]
        @pl.when(whole)
        def _():
            o_ref[...] = y

        @pl.when(jnp.logical_not(whole))
        def _():
            row = base + lax.broadcasted_iota(jnp.int32, (tm, 1), 0)
            o_ref[...] += jnp.where((row >= lo) & (row < hi), y, 0.0)


def _moe_ffn(x_packed, items, layer, w_gu, b_gu, w_down, b_down):
    a, half = x_packed.shape
    d = 2 * half
    tm = MOE_TILE_M
    n_items = items[0].shape[0]
    wmap = lambda w, tile, exp, lo, hi: (layer, exp[w], 0, 0)
    grid_spec = pltpu.PrefetchScalarGridSpec(
        num_scalar_prefetch=4,
        grid=(n_items,),
        in_specs=[
            pl.BlockSpec((tm, half), lambda w, tile, exp, lo, hi: (tile[w], 0)),
            pl.BlockSpec((1, 1, d, 2 * D_EXPERT), wmap),
            pl.BlockSpec((1, 1, 1, 2 * D_EXPERT), wmap),
            pl.BlockSpec((1, 1, D_EXPERT, d), wmap),
            pl.BlockSpec((1, 1, 1, d), wmap),
        ],
        out_specs=pl.BlockSpec((tm, d), lambda w, tile, exp, lo, hi: (tile[w], 0)),
        scratch_shapes=[pltpu.VMEM((d, 2 * D_EXPERT), BF16), pltpu.VMEM((D_EXPERT, d), BF16)],
    )
    depth = w_gu.shape[0]
    return pl.pallas_call(
        _moe_ffn_body,
        grid_spec=grid_spec,
        out_shape=jax.ShapeDtypeStruct((a, d), F32),
        compiler_params=pltpu.CompilerParams(dimension_semantics=("arbitrary",),
                                             vmem_limit_bytes=MOE_VMEM_LIMIT_BYTES),
        name="moe_ffn",
    )(*items, x_packed, w_gu, b_gu.reshape(depth, N_EXPERTS, 1, -1), w_down,
      b_down.reshape(depth, N_EXPERTS, 1, -1))


def _moe(t, logits, layer, w_gu, b_gu, w_down, b_down):
    n, d = t.shape
    a = n * TOP_K
    tm = MOE_TILE_M
    top_v, top_i = lax.top_k(logits, TOP_K)
    probs = jax.nn.softmax(top_v, axis=-1)

    flat_e = top_i.reshape(a).astype(jnp.int32)
    ids = jnp.arange(a, dtype=jnp.int32)
    order = jnp.sort(flat_e * a + ids) % a
    _, pos = lax.sort_key_val(order, ids)
    counts = jnp.sum((flat_e[:, None] == jnp.arange(N_EXPERTS, dtype=jnp.int32)[None, :]).astype(jnp.int32), axis=0)
    ends = jnp.cumsum(counts)

    n_tiles = -(-a // tm)
    cuts =jnp.sort(jnp.concatenate([jnp.arange(n_tiles, dtype=jnp.int32) * tm, ends[:-1].astype(jnp.int32)]))
    item_lo = cuts
    item_hi = jnp.concatenate([cuts[1:], jnp.array([a], jnp.int32)])
    item_tile = jnp.minimum(item_lo // tm, n_tiles - 1)
    item_exp = jnp.minimum(jnp.sum((ends[None, :] <= item_lo[:, None]).astype(jnp.int32), axis=1), N_EXPERTS - 1)

    half = d // 2
    packed = lax.bitcast_convert_type(jnp.stack([t[:, :half], t[:, half:]], axis=-1), jnp.uint32)
    x_sorted = jnp.take(packed, order // TOP_K, axis=0, mode='clip')
    y = _moe_ffn(x_sorted, (item_tile, item_exp, item_lo, item_hi), layer, w_gu, b_gu, w_down, b_down)
    pos_slots = pos.reshape(n, TOP_K).T.reshape(a)
    y_slots = jnp.take(y, pos_slots, axis=0, mode='clip').reshape(TOP_K, n, d)
    return y_slots, probs.T.reshape(TOP_K, n, 1)


def _head_perm():
    return np.concatenate([np.arange(0, ATT_HD, 2), np.arange(1, ATT_HD, 2)])


def _prep_layer_params(lp):
    w = lp['w_in']
    perm = _head_perm()
    qperm = (np.arange(ATT_HEADS)[:, None] * ATT_HD + perm[None, :]).reshape(-1)
    kperm = (np.arange(ATT_KV_HEADS)[:, None] * ATT_HD + perm[None, :]).reshape(-1)
    o = 0
    cols = {}
    for name, width in (('dn', 2048), ('ba', 16), ('aq', 512), ('ak', 128), ('av', 128), ('cu', 1024), ('fu', 512)):
        cols[name] = w[:, o:o + width]
        o += width
    w_cat = jnp.concatenate([
        cols['dn'], jnp.pad(cols['ba'], ((0, 0), (0, LANES - 16))),
        cols['aq'][:, qperm], cols['ak'][:, kperm], cols['av'], cols['cu'], cols['fu']], axis=1).astype(BF16)
    wp = dict(lp)
    wp['w_cat'] = w_cat
    wp['att_qw'] = (jnp.tile(lp['att_q_norm_w'][perm], ATT_HEADS) * (ATT_HD ** -0.5)).reshape(1, -1)
    wp['att_kw'] = jnp.tile(lp['att_k_norm_w'][perm], ATT_KV_HEADS).reshape(1, -1)
    wp['dn_norm_w'] = lp['dn_norm_w'].reshape(1, -1)
    wp['w_gate'] = lp['w_gate'].astype(BF16)
    wp['b_gate'] = lp['b_gate'].reshape(1, -1)
    wp['w_branch'] = lp['w_branch'].astype(BF16)
    wp['w_out'] = lp['w_out'].astype(BF16)
    wp['ln1_w'] = lp['ln1_w'].reshape(1, -1)
    wp['ln1_b'] = lp['ln1_b'].reshape(1, -1)
    wp['router_w'] = jnp.pad(lp['router_w'], ((0, 0), (0, LANES - N_EXPERTS)))
    wp['router_b'] = jnp.pad(lp['router_b'], (0, LANES - N_EXPERTS)).reshape(1, -1)
    return wp


def _rope_tables(l):
    rows = l // GRID_W
    row = jnp.repeat(jnp.arange(rows), GRID_W).astype(F32)
    col = jnp.tile(jnp.arange(GRID_W), rows).astype(F32)
    half = ATT_HD // 2
    inv = ROPE_THETA ** (-jnp.arange(0, half, 2, dtype=F32) / half)
    ang = jnp.concatenate([row[:, None] * inv, col[:, None] * inv], axis=-1)
    cos, sin = jnp.cos(ang), jnp.sin(ang)
    cosf = jnp.tile(jnp.concatenate([cos, cos], axis=-1), (1, ATT_HEADS))
    sinf = jnp.tile(jnp.concatenate([-sin, sin], axis=-1), (1, ATT_HEADS))
    return cosf, sinf


def _adaln(cvec, w, b):
    m = jax.nn.silu(cvec) @ w + b
    return m.reshape(m.shape[:-1] + (6, D_MODEL))


def _mixers(xs, mod, wp, tabs, s0f, s0b, alpha):
    cosf, sinf, bd, w_fn, dft_tabs = tabs
    u_qkv, u_z, u_ba, u_att, u_conf, pq = _inproj(xs, mod[:, 1:2], mod[:, 0:1], wp['w_cat'], w_fn)
    qf, kf, vf, gb = _dn_prep(u_qkv, u_ba, wp['dn_conv_w'], wp['dn_a_log'], wp['dn_dt_bias'])
    o_f, o_b, s_f, s_b = _dn_rec(_dn_chunk(qf, kf, vf, gb), s0f, s0b)
    q_rot, kp, vp = _att_prep(u_att, cosf, sinf, wp['att_qw'], wp['att_kw'], bd)
    y_conf = _conformer(u_conf, wp['conf_dw_w'], wp['conf_dw_b'], wp['conf_ln_w'], wp['conf_ln_b'])
    y_fn = _seq_dft(pq, dft_tabs)
    return dict(o_f=o_f, o_b=o_b, z=u_z, q=q_rot, kp=kp, vp=vp, y_conf=y_conf, y_fn=y_fn, s_f=s_f, s_b=s_b)


def _trunk_layer(x, cx, c, c_ctx, lp, moe_w, layer, tabs_l, tabs_c, last, alpha):
    b, l, d = x.shape
    lc = cx.shape[1]
    wp = _prep_layer_params(lp)
    mod = _adaln(c, lp['w_ada'], lp['b_ada'])
    mod_c = jnp.broadcast_to(_adaln(c_ctx, lp['w_ada'], lp['b_ada'])[None], (b, 6, d))
    pad8 = lambda m: jnp.pad(m, ((0, 0), (0, 2), (0, 0)))
    s0 = jnp.zeros((b, DN_HEADS, DN_DK, DN_DV), F32)

    mc = _mixers(cx, mod_c, wp, tabs_c, s0, s0, alpha)
    ml = _mixers(x, mod, wp, tabs_l, mc['s_f'], mc['s_b'], alpha)
    kp = jnp.concatenate([mc['kp'], ml['kp']], axis=2)
    vp = jnp.concatenate([mc['vp'], ml['vp']], axis=2)
    y_att = _attention(ml['q'], kp, vp)
    x1, h2, logits = _merge(x, ml['o_f'], ml['o_b'], ml['z'], y_att, ml['y_conf'], ml['y_fn'], pad8(mod), wp, alpha)
    if last:
        y_slots, p_slots = _moe(h2.reshape(-1, d), logits.reshape(-1, LANES)[:, :N_EXPERTS], layer, *moe_w)
        offset = 0
    else:
        yc_att = _attention(mc['q'], mc['kp'], mc['vp'])
        cx1, hc2, logits_c = _merge(cx, mc['o_f'], mc['o_b'], mc['z'], yc_att, mc['y_conf'], mc['y_fn'],
                                    pad8(mod_c), wp, alpha)
        t_all = jnp.concatenate([hc2.reshape(-1, d), h2.reshape(-1, d)], axis=0)
        lg_all = jnp.concatenate([logits_c.reshape(-1, LANES), logits.reshape(-1, LANES)], axis=0)[:, :N_EXPERTS]
        y_slots, p_slots = _moe(t_all, lg_all, layer, *moe_w)
        offset = b * lc
        cx = _ln2(cx1, y_slots, p_slots, 0, mod_c[:, 5:6], lp['ln2_w'], lp['ln2_b'], alpha)
    x = _ln2(x1, y_slots, p_slots, offset, mod[:, 5:6], lp['ln2_w'], lp['ln2_b'], alpha)
    return x, cx


def _tables(l, rope):
    nq = ATT_HEADS * ATT_HD
    if rope:
        cosf, sinf = _rope_tables(l)
    else:
        cosf, sinf = jnp.ones((l, nq), F32), jnp.zeros((l, nq), F32)
    bd = jnp.kron(jnp.eye(ATT_HEADS, dtype=F32), jnp.full((ATT_HD, ATT_HD), 1.0 / ATT_HD, F32)).astype(BF16)
    return cosf, sinf, bd, _channel_dft_weight(), _seq_dft_tables(l)


def kernel(x, c, ctx, c_ctx, w_ada, b_ada, w_in, dn_conv_w, dn_a_log, dn_dt_bias, dn_norm_w, att_q_norm_w, att_k_norm_w, conf_dw_w, conf_dw_b, conf_ln_w, conf_ln_b, w_branch, w_gate, b_gate, w_out, ln1_w, ln1_b, router_w, router_b, exp_w_gu, exp_b_gu, exp_w_down, exp_b_down, ln2_w, ln2_b):
    depth = w_in.shape[0]
    alpha = (2 * depth) ** 0.25
    tabs_l = _tables(x.shape[1], True)
    tabs_c = _tables(ctx.shape[1], False)
    cx = ctx
    for i in range(depth):
        lp = dict(w_ada=w_ada[i], b_ada=b_ada[i], w_in=w_in[i], dn_conv_w=dn_conv_w[i],
                  dn_a_log=dn_a_log[i], dn_dt_bias=dn_dt_bias[i], dn_norm_w=dn_norm_w[i],
                  att_q_norm_w=att_q_norm_w[i], att_k_norm_w=att_k_norm_w[i],
                  conf_dw_w=conf_dw_w[i], conf_dw_b=conf_dw_b[i], conf_ln_w=conf_ln_w[i],
                  conf_ln_b=conf_ln_b[i], w_branch=w_branch[i], w_gate=w_gate[i], b_gate=b_gate[i],
                  w_out=w_out[i], ln1_w=ln1_w[i], ln1_b=ln1_b[i], router_w=router_w[i],
                  router_b=router_b[i], ln2_w=ln2_w[i], ln2_b=ln2_b[i])
        moe_w = (exp_w_gu, exp_b_gu, exp_w_down, exp_b_down)
        x, cx = _trunk_layer(x, cx, c, c_ctx, lp, moe_w, i, tabs_l, tabs_c, i == depth - 1, alpha)
    return x
```

```python
import functools
import math

import jax
import jax.numpy as jnp
import numpy as np
from jax import lax
from jax.experimental import pallas as pl
from jax.experimental.pallas import tpu as pltpu

F32 = jnp.float32
BF16 = jnp.bfloat16

D_MODEL = 1024
GRID_W = 64
BRANCH_W = D_MODEL // 2
N_BRANCH = 4
DN_DK = 128
DN_DV = 128
DN_HEADS = BRANCH_W // DN_DV
DN_CONV = 5
DN_CHUNK = 64
ATT_HD = 64
ATT_HEADS = BRANCH_W // ATT_HD
ATT_KV_HEADS = ATT_HEADS // 4
ROPE_THETA = 10000.0
CONF_W = BRANCH_W
CONF_K = 31
FN_GROUPS = 4
FN_GW = BRANCH_W // FN_GROUPS
N_EXPERTS = 32
TOP_K = 4
D_EXPERT = D_MODEL
SWIGLU_LIMIT = 7.0
SWIGLU_ALPHA = 1.702
LN_EPS = 1e-6
RMS_EPS = 1e-6

LANES = 128
SUBLANES = 8
HALO = 16
VMEM_LIMIT_BYTES = 52 * 1024 * 1024
MOE_VMEM_LIMIT_BYTES = 58 * 1024 * 1024
MOE_TILE_M = 512
ROW_TILE = 256
MATMUL_ROW_TILE = 512
ATT_TQ = 256
CONV_ROWS = 32
DN_REC_CHUNKS_PER_STEP = 2
DN_CHUNKS_PER_STEP = 4

C_QKV = (0, 1536)
C_Z = (1536, 2048)
C_BA = (2048, 2176)
C_ATT = (2176, 2944)
C_CONF = (2944, 3968)
C_FN = (3968, 4480)
W_IN_COLS = 4480


def _cparams(*sem):
    return pltpu.CompilerParams(dimension_semantics=sem, vmem_limit_bytes=VMEM_LIMIT_BYTES)


def _nt(a, b):
    return lax.dot_general(a, b, (((1,), (1,)), ((), ())), preferred_element_type=F32)


def _split3(x):
    hi = x.astype(BF16)
    r1 = x - hi.astype(F32)
    mid = r1.astype(BF16)
    lo = (r1 - mid.astype(F32)).astype(BF16)
    return hi, mid, lo


def _dot_exact_lhs(e, x):
    hi, mid, lo = _split3(x)
    return (jnp.dot(e, hi, preferred_element_type=F32) + jnp.dot(e, mid, preferred_element_type=F32)
            + jnp.dot(e, lo, preferred_element_type=F32))


def _dot3(a, b):
    a_hi = a.astype(BF16)
    a_lo = (a - a_hi.astype(F32)).astype(BF16)
    b_hi = b.astype(BF16)
    b_lo = (b - b_hi.astype(F32)).astype(BF16)
    return (jnp.dot(a_hi, b_hi, preferred_element_type=F32) + jnp.dot(a_hi, b_lo, preferred_element_type=F32)
            + jnp.dot(a_lo, b_hi, preferred_element_type=F32))


def _stacked_dot3(lhs_list, b):
    rows = lhs_list[0].shape[0]
    his = [a.astype(BF16) for a in lhs_list]
    los = [(a - h.astype(F32)).astype(BF16) for a, h in zip(lhs_list, his)]
    b_hi = b.astype(BF16)
    b_lo = (b - b_hi.astype(F32)).astype(BF16)
    m = len(lhs_list)
    r1 = jnp.dot(jnp.concatenate(his + los, axis=0), b_hi, preferred_element_type=F32)
    r2 = jnp.dot(jnp.concatenate(his, axis=0), b_lo, preferred_element_type=F32)
    blk = lambda r, j: r[j * rows:(j + 1) * rows]
    return [blk(r1, j) + blk(r1, m + j) + blk(r2, j) for j in range(m)]


def _silu(x):
    return x * jax.nn.sigmoid(x)


def _ln_rows(x, w, b):
    mu = jnp.mean(x, axis=-1, keepdims=True)
    xc = x - mu
    var = jnp.mean(xc * xc, axis=-1, keepdims=True)
    return xc * lax.rsqrt(var + LN_EPS) * w + b


def _inproj_body(x_ref, sc_ref, sh_ref, w_ref, wfn_ref, cos_ref, sin_ref, qw_ref, kw_ref, bd_ref,
                 qkv_ref, z_ref, ba_ref, conf_ref, pq_ref, q_ref, kp_ref, vp_ref):
    h = (x_ref[0] * (1.0 + sc_ref[0]) + sh_ref[0]).astype(BF16)

    def proj(c):
        return jnp.dot(h, w_ref[:, c[0]:c[1]], preferred_element_type=F32)

    qkv_ref[0] = proj(C_QKV).astype(BF16)
    z_ref[0] = proj(C_Z).astype(BF16)
    ba_ref[0] = proj(C_BA)
    conf_ref[0] = proj(C_CONF).astype(BF16)
    fu = proj(C_FN).astype(BF16)
    pq_ref[0] = jnp.dot(fu, wfn_ref[...], preferred_element_type=F32).astype(BF16)
    _att_prep(proj(C_ATT), cos_ref, sin_ref, qw_ref, kw_ref, bd_ref, q_ref, kp_ref, vp_ref)


def _resident(a):
    return pl.BlockSpec(a.shape, lambda bi, i: (0,) * a.ndim, pipeline_mode=pl.Buffered(1))


def _inproj(x, sc, sh, w_cat, w_fn, cosf, sinf, qw, kw, bd):
    b, l, d = x.shape
    tl = min(MATMUL_ROW_TILE, l)
    nq = ATT_HEADS * ATT_HD
    nk = ATT_KV_HEADS * ATT_HD
    widths = [c[1] - c[0] for c in (C_QKV, C_Z, C_BA, C_CONF)] + [2 * BRANCH_W, nq]
    dtypes = [BF16, BF16, F32, BF16, BF16, BF16]
    row = lambda w: pl.BlockSpec((1, tl, w), lambda bi, i: (bi, i, 0))
    vec = pl.BlockSpec((1, 1, d), lambda bi, i: (bi, 0, 0))
    tab = pl.BlockSpec((tl, nq), lambda bi, i: (i, 0))
    placed = pl.BlockSpec((1, 4, tl, nk), lambda bi, i: (bi, 0, i, 0))
    placed_shape = jax.ShapeDtypeStruct((b, 4, l, nk), BF16)
    placed_t = pl.BlockSpec((1, 4, nk, tl), lambda bi, i: (bi, 0, 0, i))
    placed_t_shape = jax.ShapeDtypeStruct((b, 4, nk, l), BF16)
    return pl.pallas_call(
        _inproj_body,
        grid=(b, l // tl),
        in_specs=[row(d), vec, vec, _resident(w_cat), _resident(w_fn), tab, tab,
                  _resident(qw), _resident(kw), _resident(bd)],
        out_specs=[row(w) for w in widths] + [placed_t, placed],
        out_shape=[jax.ShapeDtypeStruct((b, l, w), dt) for w, dt in zip(widths, dtypes)]
                  + [placed_t_shape, placed_shape],
        compiler_params=_cparams("parallel", "parallel"),
        name="inproj",
    )(x, sc, sh, w_cat, w_fn, cosf, sinf, qw, kw, bd)


def _halo_specs(tl, l, c):
    per = tl // HALO
    last = l // HALO - 1
    prev = pl.BlockSpec((1, HALO, c), lambda bi, i: (bi, jnp.maximum(i * per - 1, 0), 0))
    cur = pl.BlockSpec((1, tl, c), lambda bi, i: (bi, i, 0))
    nxt = pl.BlockSpec((1, HALO, c), lambda bi, i: (bi, jnp.minimum((i + 1) * per, last), 0))
    return prev, cur, nxt


def _tap_shifts(taps):
    pad = (taps - 1) // 2
    return sorted({(HALO + k - pad) % SUBLANES for k in range(taps)})


def _fill_halo_buffer(buf, prev, cur, nxt, tl, taps):
    i = pl.program_id(1)
    n = pl.num_programs(1)
    rows = tl + 2 * HALO
    buf[0, HALO:HALO + tl] = cur
    buf[0, 0:HALO] = jnp.where(i > 0, prev, 0.0)
    buf[0, HALO + tl:rows] = jnp.where(i < n - 1, nxt, 0.0)
    for s in _tap_shifts(taps):
        if s:
            buf[s, 0:rows - SUBLANES] = buf[0, s:s + rows - SUBLANES]


def _depthwise_rows(buf, w_ref, r0, taps, c0, c1):
    pad = (taps - 1) // 2
    acc = None
    for k in range(taps):
        start = HALO + r0 + k - pad
        s = start % SUBLANES
        term = buf[s, start - s:start - s + CONV_ROWS, c0:c1] * w_ref[k:k + 1, c0:c1]
        acc = term if acc is None else acc + term
    return acc


def _conf_body(up_ref, u_ref, un_ref, w_ref, b_ref, lnw_ref, lnb_ref, o_ref, buf):
    tl = u_ref.shape[1]

    def glu(u):
        u = u.astype(F32)
        return u[:, :CONF_W] * jax.nn.sigmoid(u[:, CONF_W:])

    _fill_halo_buffer(buf, glu(up_ref[0]), glu(u_ref[0]), glu(un_ref[0]), tl, CONF_K)
    for r0 in range(0, tl, CONV_ROWS):
        y = _depthwise_rows(buf, w_ref, r0, CONF_K, 0, CONF_W) + b_ref[...]
        o_ref[0, r0:r0 + CONV_ROWS, :] = _silu(_ln_rows(y, lnw_ref[...], lnb_ref[...])).astype(BF16)


def _conformer(u, dw_w, dw_b, ln_w, ln_b):
    b, l, c = u.shape
    tl = min(ROW_TILE, l)
    prev, cur, nxt = _halo_specs(tl, l, c)
    full = lambda a: pl.BlockSpec(a.shape, lambda bi, i: (0,) * a.ndim)
    args = (dw_w, dw_b.reshape(1, -1), ln_w.reshape(1, -1), ln_b.reshape(1, -1))
    return pl.pallas_call(
        _conf_body,
        grid=(b, l // tl),
        in_specs=[prev, cur, nxt] + [full(a) for a in args],
        out_specs=pl.BlockSpec((1, tl, CONF_W), lambda bi, i: (bi, i, 0)),
        out_shape=jax.ShapeDtypeStruct((b, l, CONF_W), BF16),
        scratch_shapes=[pltpu.VMEM((SUBLANES, tl + 2 * HALO, CONF_W), F32)],
        compiler_params=_cparams("parallel", "parallel"),
        name="conformer",
    )(u, u, u, *args)


def _dn_prep_body(up_ref, u_ref, un_ref, ba_ref, w_ref, alog_ref, dtb_ref, q_ref, k_ref, v_ref, gb_ref, buf):
    tl = u_ref.shape[1]
    hk = DN_HEADS * DN_DK
    _fill_halo_buffer(buf, up_ref[0].astype(F32), u_ref[0].astype(F32), un_ref[0].astype(F32), tl, DN_CONV)
    for r0 in range(0, tl, CONV_ROWS):
        rows = slice(r0, r0 + CONV_ROWS)
        for hd in range(DN_HEADS):
            for part, ref in ((0, q_ref), (1, k_ref), (2, v_ref)):
                c0 = part * hk + hd * DN_DK
                y = _silu(_depthwise_rows(buf, w_ref, r0, DN_CONV, c0, c0 + DN_DK))
                if part < 2:
                    y = y * lax.rsqrt(jnp.sum(y * y, axis=-1, keepdims=True) + 1e-6)
                if part == 0:
                    y = y * (DN_DK ** -0.5)
                ref[0, rows, hd * DN_DK:(hd + 1) * DN_DK] = y
    ba = ba_ref[0]
    t = ba + dtb_ref[...]
    softplus = jnp.maximum(t, 0.0) + jnp.log1p(jnp.exp(-jnp.abs(t)))
    g = -jnp.exp(alog_ref[...]) * softplus
    lane = lax.broadcasted_iota(jnp.int32, ba.shape, 1)
    gb_ref[0] = jnp.where(lane < 2 * DN_HEADS, jax.nn.sigmoid(ba), g)


def _dn_prep(u_qkv, u_ba, conv_w, a_log, dt_bias):
    b, l, c = u_qkv.shape
    tl = min(ROW_TILE, l)
    prev, cur, nxt = _halo_specs(tl, l, c)
    nh = 2 * DN_HEADS
    alog_row = jnp.zeros((1, LANES), F32).at[0, nh:2 * nh].set(a_log.reshape(-1))
    dtb_row = jnp.zeros((1, LANES), F32).at[0, nh:2 * nh].set(dt_bias.reshape(-1))
    row = lambda w: pl.BlockSpec((1, tl, w), lambda bi, i: (bi, i, 0))
    full = lambda a: pl.BlockSpec(a.shape, lambda bi, i: (0,) * a.ndim)
    return pl.pallas_call(
        _dn_prep_body,
        grid=(b, l // tl),
        in_specs=[prev, cur, nxt, row(LANES), full(conv_w), full(alog_row), full(dtb_row)],
        out_specs=[row(BRANCH_W), row(BRANCH_W), row(BRANCH_W), row(LANES)],
        out_shape=[jax.ShapeDtypeStruct((b, l, BRANCH_W), F32)] * 3 + [jax.ShapeDtypeStruct((b, l, LANES), F32)],
        scratch_shapes=[pltpu.VMEM((SUBLANES, tl + 2 * HALO, c), F32)],
        compiler_params=_cparams("parallel", "parallel"),
        name="dn_prep",
    )(u_qkv, u_qkv, u_qkv, u_ba, conv_w, alog_row, dtb_row)


def _dn_chunk_body(q_ref, k_ref, v_ref, gb_ref, uf_ref, wqf_ref, qktf_ref, glf_ref,
                   ub_ref, wqb_ref, qktb_ref, glb_ref):
    C = DN_CHUNK
    nh = 2 * DN_HEADS
    ii = lax.broadcasted_iota(jnp.int32, (C, C), 0)
    jj = lax.broadcasted_iota(jnp.int32, (C, C), 1)
    eye = (ii == jj).astype(F32)
    eye_l = (lax.broadcasted_iota(jnp.int32, (LANES, LANES), 0)
             == lax.broadcasted_iota(jnp.int32, (LANES, LANES), 1)).astype(BF16)
    lower = ((ii >= jj), (ii <= jj))
    strict = ((ii > jj), (ii < jj))
    outs = ((uf_ref, wqf_ref, qktf_ref, glf_ref), (ub_ref, wqb_ref, qktb_ref, glb_ref))
    n_chunks = q_ref.shape[1] // C

    cum = {}
    for ck in range(n_chunks):
        gb = gb_ref[0, ck * C:(ck + 1) * C, :]
        for d in range(2):
            gcum = _dot_exact_lhs(lower[d].astype(BF16), gb)
            hi, mid, lo = _split3(gcum)
            cum[ck, d] = (gb, gcum, _nt(eye_l, hi) + _nt(eye_l, mid) + _nt(eye_l, lo))
            last = C - 1 if d == 0 else 0
            outs[d][3][0, ck * 8:(ck + 1) * 8, :] = jnp.broadcast_to(gcum[last:last + 1, :], (8, LANES))

    units = [(ck, d, hd) for ck in range(n_chunks) for d in range(2) for hd in range(DN_HEADS)]
    kk, kbs, decay, n = {}, {}, {}, {}
    for ck in range(n_chunks):
        for hd in range(DN_HEADS):
            kb = k_ref[0, ck * C:(ck + 1) * C, hd * DN_DK:(hd + 1) * DN_DK].astype(BF16)
            kbs[ck, hd] = kb
            kk[ck, hd] = _nt(kb, kb)
    for (ck, d, hd) in units:
        gb, gcum, gcum_t = cum[ck, d]
        cg = nh + d * DN_HEADS + hd
        cb = d * DN_HEADS + hd
        diff = gcum[:, cg:cg + 1] - gcum_t[cg:cg + 1, :]
        dec = jnp.where(lower[d], jnp.exp(jnp.where(lower[d], diff, 0.0)), 0.0)
        decay[ck, d, hd] = dec
        n[ck, d, hd] = jnp.where(strict[d], kk[ck, hd] * dec * gb[:, cb:cb + 1], 0.0)

    t = {u: eye - n[u] for u in units}
    p = {u: _stacked_dot3([n[u]], n[u])[0] for u in units}
    for _ in range(4):
        tp = {u: _stacked_dot3([t[u], p[u]], p[u]) for u in units}
        t = {u: t[u] + tp[u][0] for u in units}
        p = {u: tp[u][1] for u in units}
    t = {u: t[u] + _stacked_dot3([t[u]], p[u])[0] for u in units}

    for (ck, d, hd) in units:
        gb, gcum, _ = cum[ck, d]
        u_ref, wq_ref, qkt_ref, _ = outs[d]
        cg = nh + d * DN_HEADS + hd
        cb = d * DN_HEADS + hd
        rows = slice(ck * C, (ck + 1) * C)
        cols = slice(hd * DN_DK, (hd + 1) * DN_DK)
        q = q_ref[0, rows, cols]
        k = k_ref[0, rows, cols]
        v = v_ref[0, rows, cols]
        beta = gb[:, cb:cb + 1]
        gc = gcum[:, cg:cg + 1]
        last = C - 1 if d == 0 else 0
        eg = jnp.exp(gc)
        rhs = jnp.concatenate([v * beta, k * (beta * eg)], axis=-1).astype(BF16)
        sol = jnp.dot(t[ck, d, hd].astype(BF16), rhs, preferred_element_type=F32)
        u_ref[0, hd, rows, :] = sol[:, :DN_DV]
        wq_ref[0, hd, ck * 2 * C:ck * 2 * C + C, :] = sol[:, DN_DV:].astype(BF16)
        wq_ref[0, hd, ck * 2 * C + C:(ck + 1) * 2 * C, :] = (q * eg).astype(BF16)
        qk = _nt(q.astype(BF16), kbs[ck, hd]) * decay[ck, d, hd]
        k_dec = (k * jnp.exp(gc[last:last + 1, :] - gc)).astype(BF16)
        qkt_ref[0, hd, ck * 3 * C:ck * 3 * C + C, :] = qk.astype(BF16)
        qkt_ref[0, hd, ck * 3 * C + C:(ck + 1) * 3 * C, :] = _nt(eye_l, k_dec).astype(BF16)


def _dn_chunk(qf, kf, vf, gb):
    b, l, w = qf.shape
    C = DN_CHUNK
    rows = min(DN_CHUNKS_PER_STEP * C, l)
    per = rows // C
    nc = l // C
    row = lambda cw: pl.BlockSpec((1, rows, cw), lambda bi, i: (bi, i, 0))
    head = lambda r, cw: pl.BlockSpec((1, DN_HEADS, r, cw), lambda bi, i: (bi, 0, i, 0))
    one_dir_specs = [head(rows, DN_DV), head(2 * rows, DN_DK), head(3 * rows, C),
                     pl.BlockSpec((1, per * 8, LANES), lambda bi, i: (bi, i, 0))]
    one_dir_shapes = [jax.ShapeDtypeStruct((b, DN_HEADS, l, DN_DV), F32),
                      jax.ShapeDtypeStruct((b, DN_HEADS, 2 * l, DN_DK), BF16),
                      jax.ShapeDtypeStruct((b, DN_HEADS, 3 * l, C), BF16),
                      jax.ShapeDtypeStruct((b, nc * 8, LANES), F32)]
    return pl.pallas_call(
        _dn_chunk_body,
        grid=(b, l // rows),
        in_specs=[row(w), row(w), row(w), row(LANES)],
        out_specs=one_dir_specs * 2,
        out_shape=one_dir_shapes * 2,
        compiler_params=_cparams("parallel", "parallel"),
        name="dn_chunk",
    )(qf, kf, vf, gb)


def _dn_rec_body(uf_ref, wqf_ref, qktf_ref, glf_ref, ub_ref, wqb_ref, qktb_ref, glb_ref, s0f_ref, s0b_ref,
                 of_ref, ob_ref, sf_ref, sb_ref, st_f, st_b):
    c = pl.program_id(1)
    nc = pl.num_programs(1)
    C = DN_CHUNK
    nh = 2 * DN_HEADS

    @pl.when(c == 0)
    def _():
        st_f[...] = s0f_ref[0]
        st_b[...] = s0b_ref[0]

    dirs = ((uf_ref, wqf_ref, qktf_ref, glf_ref, of_ref, st_f), (ub_ref, wqb_ref, qktb_ref, glb_ref, ob_ref, st_b))
    units = [(d, hd) for d in range(2) for hd in range(DN_HEADS)]
    per = uf_ref.shape[2] // C
    s = {(d, hd): dirs[d][5][hd] for (d, hd) in units}
    for step in range(per):
        at = {0: step, 1: per - 1 - step}
        r1 = {(d, hd): jnp.dot(dirs[d][1][0, hd, at[d] * 2 * C:(at[d] + 1) * 2 * C, :], s[d, hd].astype(BF16),
                               preferred_element_type=F32) for (d, hd) in units}
        vb = {(d, hd): (dirs[d][0][0, hd, at[d] * C:(at[d] + 1) * C, :] - r1[d, hd][:C]).astype(BF16)
              for (d, hd) in units}
        r2 = {(d, hd): jnp.dot(dirs[d][2][0, hd, at[d] * 3 * C:(at[d] + 1) * 3 * C, :], vb[d, hd],
                               preferred_element_type=F32) for (d, hd) in units}
        for (d, hd) in units:
            cg = nh + d * DN_HEADS + hd
            decay_all = jnp.exp(dirs[d][3][0, at[d] * 8:at[d] * 8 + 1, cg:cg + 1])
            dirs[d][4][0, at[d] * C:(at[d] + 1) * C, hd * DN_DV:(hd + 1) * DN_DV] = r1[d, hd][C:] + r2[d, hd][:C]
            s[d, hd] = s[d, hd] * decay_all + r2[d, hd][C:]
    for (d, hd) in units:
        dirs[d][5][hd] = s[d, hd]

    @pl.when(c == nc - 1)
    def _():
        sf_ref[0] = st_f[...]
        sb_ref[0] = st_b[...]


def _dn_rec(chunk_outs, s0f, s0b):
    uf = chunk_outs[0]
    b, _, l, _ = uf.shape
    per = min(DN_REC_CHUNKS_PER_STEP, l // DN_CHUNK)
    C = per * DN_CHUNK
    nc = l // C
    fwd = lambda c: c
    bwd = lambda c: nc - 1 - c

    def dir_specs(cidx):
        head = lambda r, cw: pl.BlockSpec((1, DN_HEADS, r, cw), lambda bi, c: (bi, 0, cidx(c), 0))
        return [head(C, DN_DV), head(2 * C, DN_DK), head(3 * C, DN_CHUNK),
                pl.BlockSpec((1, per * 8, LANES), lambda bi, c: (bi, cidx(c), 0))]

    st = pl.BlockSpec((1, DN_HEADS, DN_DK, DN_DV), lambda bi, c: (bi, 0, 0, 0))
    st_shape = jax.ShapeDtypeStruct((b, DN_HEADS, DN_DK, DN_DV), F32)
    out = lambda cidx: pl.BlockSpec((1, C, BRANCH_W), lambda bi, c: (bi, cidx(c), 0))
    return pl.pallas_call(
        _dn_rec_body,
        grid=(b, nc),
        in_specs=dir_specs(fwd) + dir_specs(bwd) + [st, st],
        out_specs=[out(fwd), out(bwd), st, st],
        out_shape=[jax.ShapeDtypeStruct((b, l, BRANCH_W), F32)] * 2 + [st_shape, st_shape],
        scratch_shapes=[pltpu.VMEM((DN_HEADS, DN_DK, DN_DV), F32)] * 2,
        compiler_params=_cparams("parallel", "arbitrary"),
        name="dn_rec",
    )(*chunk_outs, s0f, s0b)


def _att_prep(a, cos_ref, sin_ref, qw_ref, kw_ref, bd_ref, q_ref, kp_ref, vp_ref):
    nq = ATT_HEADS * ATT_HD
    nk = ATT_KV_HEADS * ATT_HD

    def norm_rope(x, w, width):
        ms = jnp.dot((x * x).astype(BF16), bd_ref[:width, :width], preferred_element_type=F32)
        y = x * lax.rsqrt(ms + RMS_EPS) * w
        lane = lax.broadcasted_iota(jnp.int32, y.shape, 1)
        half = ATT_HD // 2
        partner = jnp.where(lane % ATT_HD < half, pltpu.roll(y, width - half, 1), pltpu.roll(y, half, 1))
        return y * cos_ref[:, :width] + partner * sin_ref[:, :width]

    q_ref[0] = norm_rope(a[:, :nq], qw_ref[...], nq).astype(BF16)
    k = norm_rope(a[:, nq:nq + nk], kw_ref[...], nk)
    v = a[:, nq + nk:]
    lane = lax.broadcasted_iota(jnp.int32, k.shape, 1)
    first = lane < ATT_HD
    eye_l = (lax.broadcasted_iota(jnp.int32, (LANES, LANES), 0)
             == lax.broadcasted_iota(jnp.int32, (LANES, LANES), 1)).astype(BF16)
    for x, ref, transposed in ((k, kp_ref, True), (v, vp_ref, False)):
        x0 = jnp.where(first, x, 0.0)
        x1 = jnp.where(first, 0.0, x)
        placed = (x0, pltpu.roll(x0, ATT_HD, 1), pltpu.roll(x1, ATT_HD, 1), x1)
        for j, xp in enumerate(placed):
            xp = xp.astype(BF16)
            ref[0, j] = _nt(eye_l, xp).astype(BF16) if transposed else xp


def _attn_body(q_ref, k_ref, v_ref, o_ref):
    for m in range(ATT_HEADS // 2):
        q = q_ref[0, :, m * LANES:(m + 1) * LANES]
        kv = (2 * m) // (ATT_HEADS // ATT_KV_HEADS)
        acc = None
        for half in range(2):
            s = jnp.dot(q, k_ref[0, 2 * kv + half], preferred_element_type=F32)
            p = jnp.exp(s - jnp.max(s, axis=-1, keepdims=True))
            denom = jnp.sum(p, axis=-1, keepdims=True)
            o = jnp.dot(p.astype(BF16), v_ref[0, 2 * kv + half], preferred_element_type=F32) / denom
            acc = o if acc is None else acc + o
        o_ref[0, :, m * LANES:(m + 1) * LANES] = acc.astype(BF16)


def _attention(q, kp, vp):
    b, l, nq = q.shape
    tq = min(ATT_TQ, l)
    whole = lambda a: pl.BlockSpec((1,) + a.shape[1:], lambda bi, i: (bi, 0, 0, 0))
    return pl.pallas_call(
        _attn_body,
        grid=(b, l // tq),
        in_specs=[pl.BlockSpec((1, tq, nq), lambda bi, i: (bi, i, 0)), whole(kp), whole(vp)],
        out_specs=pl.BlockSpec((1, tq, nq), lambda bi, i: (bi, i, 0)),
        out_shape=jax.ShapeDtypeStruct((b, l, nq), BF16),
        compiler_params=_cparams("parallel", "arbitrary"),
        name="attention",
    )(q, kp, vp)


def _dft_body(ca_ref, sa_ref, cb_ref, sb_ref, pq_ref, o_ref, ctab, stab):
    @pl.when(pl.program_id(1) == 0)
    def _():
        cb = cb_ref[...]
        sb = sb_ref[...]
        for j in range(ca_ref.shape[0]):
            ca = ca_ref[j:j + 1, :]
            sa = sa_ref[j:j + 1, :]
            rows = slice(j * GRID_W, (j + 1) * GRID_W)
            ctab[rows, :] = (ca * cb - sa * sb).astype(BF16)
            stab[rows, :] = (sa * cb + ca * sb).astype(BF16)

    p = pq_ref[0, :, :BRANCH_W]
    q = pq_ref[0, :, BRANCH_W:]
    y = (jnp.dot(ctab[...], p, preferred_element_type=F32)
         - jnp.dot(stab[...], q, preferred_element_type=F32))
    o_ref[0] = y.astype(BF16)


def _seq_dft(pq, tabs):
    ca, sa, cb, sb = tabs
    b, l, w2 = pq.shape
    tn = min(512, l)
    coarse = pl.BlockSpec((tn // GRID_W, l), lambda i, bi: (i, 0))
    fine = pl.BlockSpec((GRID_W, l), lambda i, bi: (0, 0))
    return pl.pallas_call(
        _dft_body,
        grid=(l // tn, b),
        in_specs=[coarse, coarse, fine, fine, pl.BlockSpec((1, l, w2), lambda i, bi: (bi, 0, 0))],
        out_specs=pl.BlockSpec((1, tn, BRANCH_W), lambda i, bi: (bi, i, 0)),
        out_shape=jax.ShapeDtypeStruct((b, l, BRANCH_W), BF16),
        scratch_shapes=[pltpu.VMEM((tn, l), BF16)] * 2,
        compiler_params=_cparams("parallel", "arbitrary"),
        name="seq_dft",
    )(ca, sa, cb, sb, pq)


def _seq_dft_tables(l):
    m = jnp.arange(l, dtype=jnp.int32)[None, :]

    def cs(rows, step, scale):
        k = (jnp.arange(rows, dtype=jnp.int32)[:, None] * step * m) % l
        ang = k.astype(F32) * (2.0 * np.pi / l)
        return jnp.cos(ang) * scale, jnp.sin(ang) * scale

    ca, sa = cs(l // GRID_W, GRID_W, l ** -0.5)
    cb, sb = cs(GRID_W, 1, 1.0)
    return ca, sa, cb, sb


def _dft_tables(n, scale_dtype=BF16):
    k = (jnp.arange(n, dtype=jnp.int32)[:, None] * jnp.arange(n, dtype=jnp.int32)[None, :]) % n
    ang = k.astype(F32) * (2.0 * np.pi / n)
    s = n ** -0.5
    return (jnp.cos(ang) * s).astype(scale_dtype), (jnp.sin(ang) * s).astype(scale_dtype)


def _channel_dft_weight():
    c, s = _dft_tables(FN_GW, F32)
    eye = jnp.eye(FN_GROUPS, dtype=F32)
    return jnp.concatenate([jnp.kron(eye, c), jnp.kron(eye, s)], axis=1).astype(BF16)


def _merge_body(x_ref, of_ref, ob_ref, z_ref, ya_ref, yc_ref, yf_ref, mod_ref, nw_ref, wg_ref, bg_ref,
                wb_ref, wo_ref, lnw_ref, lnb_ref, rw_ref, rb_ref, x1_ref, h2_ref, lg_ref, *, alpha):
    x = x_ref[0]
    sh1, sc1, g1, sh2, sc2 = (mod_ref[0, j:j + 1, :] for j in range(5))
    h = (x * (1.0 + sc1) + sh1).astype(BF16)
    o = of_ref[0] + ob_ref[0]
    z = z_ref[0].astype(F32)
    parts = []
    for hd in range(DN_HEADS):
        cols = slice(hd * DN_DV, (hd + 1) * DN_DV)
        oh = o[:, cols]
        oh = oh * lax.rsqrt(jnp.mean(oh * oh, axis=-1, keepdims=True) + RMS_EPS) * nw_ref[...]
        parts.append(oh * _silu(z[:, cols]))
    y_dn = jnp.concatenate(parts, axis=-1).astype(BF16)
    ys = (y_dn, ya_ref[0], yc_ref[0], yf_ref[0])
    m = None
    for j in range(N_BRANCH):
        cols = slice(j * D_MODEL, (j + 1) * D_MODEL)
        gate = jax.nn.sigmoid(jnp.dot(h, wg_ref[:, cols], preferred_element_type=F32) + bg_ref[:, cols])
        term = gate * jnp.dot(ys[j], wb_ref[j], preferred_element_type=F32)
        m = term if m is None else m + term
    out = jnp.dot(m.astype(BF16), wo_ref[...], preferred_element_type=F32)
    x1 = _ln_rows(alpha * x + g1 * out, lnw_ref[...], lnb_ref[...])
    x1_ref[0] = x1
    h2 = x1 * (1.0 + sc2) + sh2
    h2_ref[0] = h2.astype(BF16)
    lg_ref[0] = _dot3(h2, rw_ref[...]) + rb_ref[...]


def _merge(x, o_f, o_b, z, y_att, y_conf, y_fn, mod, wp, alpha):
    b, l, d = x.shape
    tl = min(MATMUL_ROW_TILE, l)
    row = lambda w: pl.BlockSpec((1, tl, w), lambda bi, i: (bi, i, 0))
    weights = (wp['dn_norm_w'], wp['w_gate'], wp['b_gate'], wp['w_branch'], wp['w_out'],
               wp['ln1_w'], wp['ln1_b'], wp['router_w'], wp['router_b'])
    return pl.pallas_call(
        functools.partial(_merge_body, alpha=alpha),
        grid=(b, l // tl),
        in_specs=[row(d), row(BRANCH_W), row(BRANCH_W), row(BRANCH_W), row(BRANCH_W), row(BRANCH_W),
                  row(BRANCH_W), pl.BlockSpec((1, 8, d), lambda bi, i: (bi, 0, 0))]
                 + [_resident(a) for a in weights],
        out_specs=[row(d), row(d), row(LANES)],
        out_shape=[jax.ShapeDtypeStruct((b, l, d), F32), jax.ShapeDtypeStruct((b, l, d), BF16),
                   jax.ShapeDtypeStruct((b, l, LANES), F32)],
        compiler_params=_cparams("parallel", "parallel"),
        name="merge",
    )(x, o_f, o_b, z, y_att, y_conf, y_fn, mod, *weights)


def _ln2_body(x_ref, y_ref, p_ref, g_ref, w_ref, b_ref, o_ref, *, alpha):
    f = y_ref[0] * p_ref[0]
    for k in range(1, TOP_K):
        f = f + y_ref[k] * p_ref[k]
    o_ref[0] = _ln_rows(alpha * x_ref[0] + g_ref[0] * f, w_ref[...], b_ref[...])


def _ln2(x1, y_slots, p_slots, token_offset, g2, w, bias, alpha):
    b, l, d = x1.shape
    tl = math.gcd(min(ROW_TILE, l), token_offset) if token_offset else min(ROW_TILE, l)
    per_b = l // tl
    off = token_offset // tl
    row = pl.BlockSpec((1, tl, d), lambda bi, i: (bi, i, 0))
    slots = lambda cw: pl.BlockSpec((TOP_K, tl, cw), lambda bi, i: (0, off + bi * per_b + i, 0))
    vec = pl.BlockSpec((1, 1, d), lambda bi, i: (bi, 0, 0))
    par = pl.BlockSpec((1, d), lambda bi, i: (0, 0))
    return pl.pallas_call(
        functools.partial(_ln2_body, alpha=alpha),
        grid=(b, per_b),
        in_specs=[row, slots(d), slots(1), vec, par, par],
        out_specs=row,
        out_shape=jax.ShapeDtypeStruct((b, l, d), F32),
        compiler_params=_cparams("parallel", "parallel"),
        name="ln2",
    )(x1, y_slots, p_slots, g2, w.reshape(1, d), bias.reshape(1, d))


def _moe_ffn_body(tile_ref, exp_ref, lo_ref, hi_ref, x_ref, wgu_ref, bgu_ref, wd_ref, bd_ref, o_ref,
                  wgu_bf, wd_bf):
    w = pl.program_id(0)
    tm = o_ref.shape[0]
    e = exp_ref[w]
    e_prev = exp_ref[jnp.maximum(w - 1, 0)]
    lo = lo_ref[w]
    hi = hi_ref[w]
    base = tile_ref[w] * tm

    @pl.when((w == 0) | (e != e_prev))
    def _():
        wgu_bf[...] = wgu_ref[0, 0].astype(BF16)
        wd_bf[...] = wd_ref[0, 0].astype(BF16)

    whole = (lo == base) & (hi == base + tm)

    @pl.when((lo == base) & jnp.logical_not(whole))
    def _():
        o_ref[...] = jnp.zeros_like(o_ref)

    @pl.when(hi > lo)
    def _():
        packed = x_ref[...]
        x_lo = lax.bitcast_convert_type(packed << 16, F32)
        x_hi = lax.bitcast_convert_type(packed & jnp.uint32(0xFFFF0000), F32)
        x = jnp.concatenate([x_lo, x_hi], axis=-1).astype(BF16)
        gu = jnp.dot(x, wgu_bf[...], preferred_element_type=F32) + bgu_ref[0, 0]
        gate = jnp.minimum(gu[:, :D_EXPERT], SWIGLU_LIMIT)
        up = jnp.clip(gu[:, D_EXPERT:], -SWIGLU_LIMIT, SWIGLU_LIMIT)
        act = (up + 1.0) * gate * jax.nn.sigmoid(SWIGLU_ALPHA * gate)
        y = jnp.dot(act.astype(BF16), wd_bf[...], preferred_element_type=F32) + bd_ref[0, 0]
        @pl.when(whole)
        def _():
            o_ref[...] = y

        @pl.when(jnp.logical_not(whole))
        def _():
            row = base + lax.broadcasted_iota(jnp.int32, (tm, 1), 0)
            o_ref[...] += jnp.where((row >= lo) & (row < hi), y, 0.0)


def _moe_ffn(x_packed, items, layer, w_gu, b_gu, w_down, b_down):
    a, half = x_packed.shape
    d = 2 * half
    tm = MOE_TILE_M
    n_items = items[0].shape[0]
    wmap = lambda w, tile, exp, lo, hi: (layer, exp[w], 0, 0)
    grid_spec = pltpu.PrefetchScalarGridSpec(
        num_scalar_prefetch=4,
        grid=(n_items,),
        in_specs=[
            pl.BlockSpec((tm, half), lambda w, tile, exp, lo, hi: (tile[w], 0)),
            pl.BlockSpec((1, 1, d, 2 * D_EXPERT), wmap),
            pl.BlockSpec((1, 1, 1, 2 * D_EXPERT), wmap),
            pl.BlockSpec((1, 1, D_EXPERT, d), wmap),
            pl.BlockSpec((1, 1, 1, d), wmap),
        ],
        out_specs=pl.BlockSpec((tm, d), lambda w, tile, exp, lo, hi: (tile[w], 0)),
        scratch_shapes=[pltpu.VMEM((d, 2 * D_EXPERT), BF16), pltpu.VMEM((D_EXPERT, d), BF16)],
    )
    depth = w_gu.shape[0]
    return pl.pallas_call(
        _moe_ffn_body,
        grid_spec=grid_spec,
        out_shape=jax.ShapeDtypeStruct((a, d), F32),
        compiler_params=pltpu.CompilerParams(dimension_semantics=("arbitrary",),
                                             vmem_limit_bytes=MOE_VMEM_LIMIT_BYTES),
        name="moe_ffn",
    )(*items, x_packed, w_gu, b_gu.reshape(depth, N_EXPERTS, 1, -1), w_down,
      b_down.reshape(depth, N_EXPERTS, 1, -1))


def _moe(t, logits, layer, w_gu, b_gu, w_down, b_down):
    n, d = t.shape
    a = n * TOP_K
    tm = MOE_TILE_M
    top_v, top_i = lax.top_k(logits, TOP_K)
    probs = jax.nn.softmax(top_v, axis=-1)

    flat_e = top_i.reshape(a).astype(jnp.int32)
    ids = jnp.arange(a, dtype=jnp.int32)
    order = jnp.sort(flat_e * a + ids) % a
    _, pos = lax.sort_key_val(order, ids)
    counts = jnp.sum((flat_e[:, None] == jnp.arange(N_EXPERTS, dtype=jnp.int32)[None, :]).astype(jnp.int32), axis=0)
    ends = jnp.cumsum(counts)

    n_tiles = -(-a // tm)
    cuts =jnp.sort(jnp.concatenate([jnp.arange(n_tiles, dtype=jnp.int32) * tm, ends[:-1].astype(jnp.int32)]))
    item_lo = cuts
    item_hi = jnp.concatenate([cuts[1:], jnp.array([a], jnp.int32)])
    item_tile = jnp.minimum(item_lo // tm, n_tiles - 1)
    item_exp = jnp.minimum(jnp.sum((ends[None, :] <= item_lo[:, None]).astype(jnp.int32), axis=1), N_EXPERTS - 1)

    half = d // 2
    packed = lax.bitcast_convert_type(jnp.stack([t[:, :half], t[:, half:]], axis=-1), jnp.uint32)
    x_sorted = jnp.take(packed, order // TOP_K, axis=0, mode='clip')
    y = _moe_ffn(x_sorted, (item_tile, item_exp, item_lo, item_hi), layer, w_gu, b_gu, w_down, b_down)
    pos_slots = pos.reshape(n, TOP_K).T.reshape(a)
    y_slots = jnp.take(y, pos_slots, axis=0, mode='clip').reshape(TOP_K, n, d)
    return y_slots, probs.T.reshape(TOP_K, n, 1)


def _head_perm():
    return np.concatenate([np.arange(0, ATT_HD, 2), np.arange(1, ATT_HD, 2)])


def _prep_layer_params(lp):
    w = lp['w_in']
    perm = _head_perm()
    qperm = (np.arange(ATT_HEADS)[:, None] * ATT_HD + perm[None, :]).reshape(-1)
    kperm = (np.arange(ATT_KV_HEADS)[:, None] * ATT_HD + perm[None, :]).reshape(-1)
    o = 0
    cols = {}
    for name, width in (('dn', 2048), ('ba', 16), ('aq', 512), ('ak', 128), ('av', 128), ('cu', 1024), ('fu', 512)):
        cols[name] = w[:, o:o + width]
        o += width
    w_cat = jnp.concatenate([
        cols['dn'], jnp.pad(cols['ba'], ((0, 0), (0, LANES - 16))),
        cols['aq'][:, qperm], cols['ak'][:, kperm], cols['av'], cols['cu'], cols['fu']], axis=1).astype(BF16)
    wp = dict(lp)
    wp['w_cat'] = w_cat
    wp['att_qw'] = (jnp.tile(lp['att_q_norm_w'][perm], ATT_HEADS) * (ATT_HD ** -0.5)).reshape(1, -1)
    wp['att_kw'] = jnp.tile(lp['att_k_norm_w'][perm], ATT_KV_HEADS).reshape(1, -1)
    wp['dn_norm_w'] = lp['dn_norm_w'].reshape(1, -1)
    wp['w_gate'] = lp['w_gate'].astype(BF16)
    wp['b_gate'] = lp['b_gate'].reshape(1, -1)
    wp['w_branch'] = lp['w_branch'].astype(BF16)
    wp['w_out'] = lp['w_out'].astype(BF16)
    wp['ln1_w'] = lp['ln1_w'].reshape(1, -1)
    wp['ln1_b'] = lp['ln1_b'].reshape(1, -1)
    wp['router_w'] = jnp.pad(lp['router_w'], ((0, 0), (0, LANES - N_EXPERTS)))
    wp['router_b'] = jnp.pad(lp['router_b'], (0, LANES - N_EXPERTS)).reshape(1, -1)
    return wp


def _rope_tables(l):
    rows = l // GRID_W
    row = jnp.repeat(jnp.arange(rows), GRID_W).astype(F32)
    col = jnp.tile(jnp.arange(GRID_W), rows).astype(F32)
    half = ATT_HD // 2
    inv = ROPE_THETA ** (-jnp.arange(0, half, 2, dtype=F32) / half)
    ang = jnp.concatenate([row[:, None] * inv, col[:, None] * inv], axis=-1)
    cos, sin = jnp.cos(ang), jnp.sin(ang)
    cosf = jnp.tile(jnp.concatenate([cos, cos], axis=-1), (1, ATT_HEADS))
    sinf = jnp.tile(jnp.concatenate([-sin, sin], axis=-1), (1, ATT_HEADS))
    return cosf, sinf


def _adaln(cvec, w, b):
    m = jax.nn.silu(cvec) @ w + b
    return m.reshape(m.shape[:-1] + (6, D_MODEL))


def _mixers(xs, mod, wp, tabs, s0f, s0b, alpha):
    cosf, sinf, bd, w_fn, dft_tabs = tabs
    u_qkv, u_z, u_ba, u_conf, pq, q_rot, kp, vp = _inproj(
        xs, mod[:, 1:2], mod[:, 0:1], wp['w_cat'], w_fn, cosf, sinf, wp['att_qw'], wp['att_kw'], bd)
    qf, kf, vf, gb = _dn_prep(u_qkv, u_ba, wp['dn_conv_w'], wp['dn_a_log'], wp['dn_dt_bias'])
    o_f, o_b, s_f, s_b = _dn_rec(_dn_chunk(qf, kf, vf, gb), s0f, s0b)
    y_conf = _conformer(u_conf, wp['conf_dw_w'], wp['conf_dw_b'], wp['conf_ln_w'], wp['conf_ln_b'])
    y_fn = _seq_dft(pq, dft_tabs)
    return dict(o_f=o_f, o_b=o_b, z=u_z, q=q_rot, kp=kp, vp=vp, y_conf=y_conf, y_fn=y_fn, s_f=s_f, s_b=s_b)


def _trunk_layer(x, cx, c, c_ctx, lp, moe_w, layer, tabs_l, tabs_c, last, alpha):
    b, l, d = x.shape
    lc = cx.shape[1]
    wp = _prep_layer_params(lp)
    mod = _adaln(c, lp['w_ada'], lp['b_ada'])
    mod_c = jnp.broadcast_to(_adaln(c_ctx, lp['w_ada'], lp['b_ada'])[None], (b, 6, d))
    pad8 = lambda m: jnp.pad(m, ((0, 0), (0, 2), (0, 0)))
    s0 = jnp.zeros((b, DN_HEADS, DN_DK, DN_DV), F32)

    mc = _mixers(cx, mod_c, wp, tabs_c, s0, s0, alpha)
    ml = _mixers(x, mod, wp, tabs_l, mc['s_f'], mc['s_b'], alpha)
    kp = jnp.concatenate([mc['kp'], ml['kp']], axis=3)
    vp = jnp.concatenate([mc['vp'], ml['vp']], axis=2)
    y_att = _attention(ml['q'], kp, vp)
    x1, h2, logits = _merge(x, ml['o_f'], ml['o_b'], ml['z'], y_att, ml['y_conf'], ml['y_fn'], pad8(mod), wp, alpha)
    if last:
        y_slots, p_slots = _moe(h2.reshape(-1, d), logits.reshape(-1, LANES)[:, :N_EXPERTS], layer, *moe_w)
        offset = 0
    else:
        yc_att = _attention(mc['q'], mc['kp'], mc['vp'])
        cx1, hc2, logits_c = _merge(cx, mc['o_f'], mc['o_b'], mc['z'], yc_att, mc['y_conf'], mc['y_fn'],
                                    pad8(mod_c), wp, alpha)
        t_all = jnp.concatenate([hc2.reshape(-1, d), h2.reshape(-1, d)], axis=0)
        lg_all = jnp.concatenate([logits_c.reshape(-1, LANES), logits.reshape(-1, LANES)], axis=0)[:, :N_EXPERTS]
        y_slots, p_slots = _moe(t_all, lg_all, layer, *moe_w)
        offset = b * lc
        cx = _ln2(cx1, y_slots, p_slots, 0, mod_c[:, 5:6], lp['ln2_w'], lp['ln2_b'], alpha)
    x = _ln2(x1, y_slots, p_slots, offset, mod[:, 5:6], lp['ln2_w'], lp['ln2_b'], alpha)
    return x, cx


def _tables(l, rope):
    nq = ATT_HEADS * ATT_HD
    if rope:
        cosf, sinf = _rope_tables(l)
    else:
        cosf, sinf = jnp.ones((l, nq), F32), jnp.zeros((l, nq), F32)
    bd = jnp.kron(jnp.eye(ATT_HEADS, dtype=F32), jnp.full((ATT_HD, ATT_HD), 1.0 / ATT_HD, F32)).astype(BF16)
    return cosf, sinf, bd, _channel_dft_weight(), _seq_dft_tables(l)


def kernel(x, c, ctx, c_ctx, w_ada, b_ada, w_in, dn_conv_w, dn_a_log, dn_dt_bias, dn_norm_w, att_q_norm_w, att_k_norm_w, conf_dw_w, conf_dw_b, conf_ln_w, conf_ln_b, w_branch, w_gate, b_gate, w_out, ln1_w, ln1_b, router_w, router_b, exp_w_gu, exp_b_gu, exp_w_down, exp_b_down, ln2_w, ln2_b):
    depth = w_in.shape[0]
    alpha = (2 * depth) ** 0.25
    tabs_l = _tables(x.shape[1], True)
    tabs_c = _tables(ctx.shape[1], False)
    cx = ctx
    for i in range(depth):
        lp = dict(w_ada=w_ada[i], b_ada=b_ada[i], w_in=w_in[i], dn_conv_w=dn_conv_w[i],
                  dn_a_log=dn_a_log[i], dn_dt_bias=dn_dt_bias[i], dn_norm_w=dn_norm_w[i],
                  att_q_norm_w=att_q_norm_w[i], att_k_norm_w=att_k_norm_w[i],
                  conf_dw_w=conf_dw_w[i], conf_dw_b=conf_dw_b[i], conf_ln_w=conf_ln_w[i],
                  conf_ln_b=conf_ln_b[i], w_branch=w_branch[i], w_gate=w_gate[i], b_gate=b_gate[i],
                  w_out=w_out[i], ln1_w=ln1_w[i], ln1_b=ln1_b[i], router_w=router_w[i],
                  router_b=router_b[i], ln2_w=ln2_w[i], ln2_b=ln2_b[i])
        moe_w = (exp_w_gu, exp_b_gu, exp_w_down, exp_b_down)
        x, cx = _trunk_layer(x, cx, c, c_ctx, lp, moe_w, i, tabs_l, tabs_c, i == depth - 1, alpha)
    return x
```

```python
import functools
import math

import jax
import jax.numpy as jnp
import numpy as np
from jax import lax
from jax.experimental import pallas as pl
from jax.experimental.pallas import tpu as pltpu

F32 = jnp.float32
BF16 = jnp.bfloat16

D_MODEL = 1024
GRID_W = 64
BRANCH_W = D_MODEL // 2
N_BRANCH = 4
DN_DK = 128
DN_DV = 128
DN_HEADS = BRANCH_W // DN_DV
DN_CONV = 5
DN_CHUNK = 64
ATT_HD = 64
ATT_HEADS = BRANCH_W // ATT_HD
ATT_KV_HEADS = ATT_HEADS // 4
ROPE_THETA = 10000.0
CONF_W = BRANCH_W
CONF_K = 31
FN_GROUPS = 4
FN_GW = BRANCH_W // FN_GROUPS
N_EXPERTS = 32
TOP_K = 4
D_EXPERT = D_MODEL
SWIGLU_LIMIT = 7.0
SWIGLU_ALPHA = 1.702
LN_EPS = 1e-6
RMS_EPS = 1e-6

LANES = 128
SUBLANES = 8
HALO = 16
VMEM_LIMIT_BYTES = 52 * 1024 * 1024
MOE_VMEM_LIMIT_BYTES = 58 * 1024 * 1024
MOE_TILE_M = 512
ROW_TILE = 256
MATMUL_ROW_TILE = 512
ATT_TQ = 512
CONV_ROWS = 32
DN_REC_CHUNKS_PER_STEP = 2
DN_CHUNKS_PER_STEP = 4

C_QKV = (0, 1536)
C_Z = (1536, 2048)
C_BA = (2048, 2176)
C_ATT = (2176, 2944)
C_CONF = (2944, 3968)
C_FN = (3968, 4480)
W_IN_COLS = 4480


def _cparams(*sem):
    return pltpu.CompilerParams(dimension_semantics=sem, vmem_limit_bytes=VMEM_LIMIT_BYTES)


def _nt(a, b):
    return lax.dot_general(a, b, (((1,), (1,)), ((), ())), preferred_element_type=F32)


def _split3(x):
    hi = x.astype(BF16)
    r1 = x - hi.astype(F32)
    mid = r1.astype(BF16)
    lo = (r1 - mid.astype(F32)).astype(BF16)
    return hi, mid, lo


def _dot_exact_lhs(e, x):
    hi, mid, lo = _split3(x)
    return (jnp.dot(e, hi, preferred_element_type=F32) + jnp.dot(e, mid, preferred_element_type=F32)
            + jnp.dot(e, lo, preferred_element_type=F32))


def _dot3(a, b):
    a_hi = a.astype(BF16)
    a_lo = (a - a_hi.astype(F32)).astype(BF16)
    b_hi = b.astype(BF16)
    b_lo = (b - b_hi.astype(F32)).astype(BF16)
    return (jnp.dot(a_hi, b_hi, preferred_element_type=F32) + jnp.dot(a_hi, b_lo, preferred_element_type=F32)
            + jnp.dot(a_lo, b_hi, preferred_element_type=F32))


def _stacked_dot3(lhs_list, b):
    rows = lhs_list[0].shape[0]
    his = [a.astype(BF16) for a in lhs_list]
    los = [(a - h.astype(F32)).astype(BF16) for a, h in zip(lhs_list, his)]
    b_hi = b.astype(BF16)
    b_lo = (b - b_hi.astype(F32)).astype(BF16)
    m = len(lhs_list)
    r1 = jnp.dot(jnp.concatenate(his + los, axis=0), b_hi, preferred_element_type=F32)
    r2 = jnp.dot(jnp.concatenate(his, axis=0), b_lo, preferred_element_type=F32)
    blk = lambda r, j: r[j * rows:(j + 1) * rows]
    return [blk(r1, j) + blk(r1, m + j) + blk(r2, j) for j in range(m)]


def _silu(x):
    return x * jax.nn.sigmoid(x)


def _ln_rows(x, w, b):
    mu = jnp.mean(x, axis=-1, keepdims=True)
    xc = x - mu
    var = jnp.mean(xc * xc, axis=-1, keepdims=True)
    return xc * lax.rsqrt(var + LN_EPS) * w + b


def _inproj_body(x_ref, sc_ref, sh_ref, w_ref, wfn_ref, cos_ref, sin_ref, qw_ref, kw_ref, bd_ref,
                 qkv_ref, z_ref, ba_ref, conf_ref, pq_ref, q_ref, kp_ref, vp_ref):
    h = (x_ref[0] * (1.0 + sc_ref[0]) + sh_ref[0]).astype(BF16)

    def proj(c):
        return jnp.dot(h, w_ref[:, c[0]:c[1]], preferred_element_type=F32)

    qkv_ref[0] = proj(C_QKV).astype(BF16)
    z_ref[0] = proj(C_Z).astype(BF16)
    ba_ref[0] = proj(C_BA)
    conf_ref[0] = proj(C_CONF).astype(BF16)
    fu = proj(C_FN).astype(BF16)
    pq_ref[0] = jnp.dot(fu, wfn_ref[...], preferred_element_type=F32).astype(BF16)
    _att_prep(proj(C_ATT), cos_ref, sin_ref, qw_ref, kw_ref, bd_ref, q_ref, kp_ref, vp_ref)


def _resident(a):
    return pl.BlockSpec(a.shape, lambda bi, i: (0,) * a.ndim, pipeline_mode=pl.Buffered(1))


def _inproj(x, sc, sh, w_cat, w_fn, cosf, sinf, qw, kw, bd):
    b, l, d = x.shape
    tl = min(MATMUL_ROW_TILE, l)
    nq = ATT_HEADS * ATT_HD
    nk = ATT_KV_HEADS * ATT_HD
    widths = [c[1] - c[0] for c in (C_QKV, C_Z, C_BA, C_CONF)] + [2 * BRANCH_W, nq]
    dtypes = [BF16, BF16, F32, BF16, BF16, BF16]
    row = lambda w: pl.BlockSpec((1, tl, w), lambda bi, i: (bi, i, 0))
    vec = pl.BlockSpec((1, 1, d), lambda bi, i: (bi, 0, 0))
    tab = pl.BlockSpec((tl, nq), lambda bi, i: (i, 0))
    placed = pl.BlockSpec((1, 4, tl, nk), lambda bi, i: (bi, 0, i, 0))
    placed_shape = jax.ShapeDtypeStruct((b, 4, l, nk), BF16)
    placed_t = pl.BlockSpec((1, 4, nk, tl), lambda bi, i: (bi, 0, 0, i))
    placed_t_shape = jax.ShapeDtypeStruct((b, 4, nk, l), BF16)
    return pl.pallas_call(
        _inproj_body,
        grid=(b, l // tl),
        in_specs=[row(d), vec, vec, _resident(w_cat), _resident(w_fn), tab, tab,
                  _resident(qw), _resident(kw), _resident(bd)],
        out_specs=[row(w) for w in widths] + [placed_t, placed],
        out_shape=[jax.ShapeDtypeStruct((b, l, w), dt) for w, dt in zip(widths, dtypes)]
                  + [placed_t_shape, placed_shape],
        compiler_params=_cparams("parallel", "parallel"),
        name="inproj",
    )(x, sc, sh, w_cat, w_fn, cosf, sinf, qw, kw, bd)


def _halo_specs(tl, l, c):
    per = tl // HALO
    last = l // HALO - 1
    prev = pl.BlockSpec((1, HALO, c), lambda bi, i: (bi, jnp.maximum(i * per - 1, 0), 0))
    cur = pl.BlockSpec((1, tl, c), lambda bi, i: (bi, i, 0))
    nxt = pl.BlockSpec((1, HALO, c), lambda bi, i: (bi, jnp.minimum((i + 1) * per, last), 0))
    return prev, cur, nxt


def _tap_shifts(taps):
    pad = (taps - 1) // 2
    return sorted({(HALO + k - pad) % SUBLANES for k in range(taps)})


def _fill_halo_buffer(buf, prev, cur, nxt, tl, taps):
    i = pl.program_id(1)
    n = pl.num_programs(1)
    rows = tl + 2 * HALO
    buf[0, HALO:HALO + tl] = cur
    buf[0, 0:HALO] = jnp.where(i > 0, prev, 0.0)
    buf[0, HALO + tl:rows] = jnp.where(i < n - 1, nxt, 0.0)
    for s in _tap_shifts(taps):
        if s:
            buf[s, 0:rows - SUBLANES] = buf[0, s:s + rows - SUBLANES]


def _depthwise_rows(buf, w_ref, r0, taps, c0, c1):
    pad = (taps - 1) // 2
    acc = None
    for k in range(taps):
        start = HALO + r0 + k - pad
        s = start % SUBLANES
        term = buf[s, start - s:start - s + CONV_ROWS, c0:c1] * w_ref[k:k + 1, c0:c1]
        acc = term if acc is None else acc + term
    return acc


def _conf_body(up_ref, u_ref, un_ref, w_ref, b_ref, lnw_ref, lnb_ref, o_ref, buf):
    tl = u_ref.shape[1]

    def glu(u):
        u = u.astype(F32)
        return u[:, :CONF_W] * jax.nn.sigmoid(u[:, CONF_W:])

    _fill_halo_buffer(buf, glu(up_ref[0]), glu(u_ref[0]), glu(un_ref[0]), tl, CONF_K)
    for r0 in range(0, tl, CONV_ROWS):
        y = _depthwise_rows(buf, w_ref, r0, CONF_K, 0, CONF_W) + b_ref[...]
        o_ref[0, r0:r0 + CONV_ROWS, :] = _silu(_ln_rows(y, lnw_ref[...], lnb_ref[...])).astype(BF16)


def _conformer(u, dw_w, dw_b, ln_w, ln_b):
    b, l, c = u.shape
    tl = min(ROW_TILE, l)
    prev, cur, nxt = _halo_specs(tl, l, c)
    full = lambda a: pl.BlockSpec(a.shape, lambda bi, i: (0,) * a.ndim)
    args = (dw_w, dw_b.reshape(1, -1), ln_w.reshape(1, -1), ln_b.reshape(1, -1))
    return pl.pallas_call(
        _conf_body,
        grid=(b, l // tl),
        in_specs=[prev, cur, nxt] + [full(a) for a in args],
        out_specs=pl.BlockSpec((1, tl, CONF_W), lambda bi, i: (bi, i, 0)),
        out_shape=jax.ShapeDtypeStruct((b, l, CONF_W), BF16),
        scratch_shapes=[pltpu.VMEM((SUBLANES, tl + 2 * HALO, CONF_W), F32)],
        compiler_params=_cparams("parallel", "parallel"),
        name="conformer",
    )(u, u, u, *args)


def _dn_prep_body(up_ref, u_ref, un_ref, ba_ref, w_ref, alog_ref, dtb_ref, q_ref, k_ref, v_ref, gb_ref, buf):
    tl = u_ref.shape[1]
    hk = DN_HEADS * DN_DK
    _fill_halo_buffer(buf, up_ref[0].astype(F32), u_ref[0].astype(F32), un_ref[0].astype(F32), tl, DN_CONV)
    for r0 in range(0, tl, CONV_ROWS):
        rows = slice(r0, r0 + CONV_ROWS)
        for hd in range(DN_HEADS):
            for part, ref in ((0, q_ref), (1, k_ref), (2, v_ref)):
                c0 = part * hk + hd * DN_DK
                y = _silu(_depthwise_rows(buf, w_ref, r0, DN_CONV, c0, c0 + DN_DK))
                if part < 2:
                    y = y * lax.rsqrt(jnp.sum(y * y, axis=-1, keepdims=True) + 1e-6)
                if part == 0:
                    y = y * (DN_DK ** -0.5)
                ref[0, rows, hd * DN_DK:(hd + 1) * DN_DK] = y
    ba = ba_ref[0]
    t = ba + dtb_ref[...]
    softplus = jnp.maximum(t, 0.0) + jnp.log1p(jnp.exp(-jnp.abs(t)))
    g = -jnp.exp(alog_ref[...]) * softplus
    lane = lax.broadcasted_iota(jnp.int32, ba.shape, 1)
    gb_ref[0] = jnp.where(lane < 2 * DN_HEADS, jax.nn.sigmoid(ba), g)


def _dn_prep(u_qkv, u_ba, conv_w, a_log, dt_bias):
    b, l, c = u_qkv.shape
    tl = min(ROW_TILE, l)
    prev, cur, nxt = _halo_specs(tl, l, c)
    nh = 2 * DN_HEADS
    alog_row = jnp.zeros((1, LANES), F32).at[0, nh:2 * nh].set(a_log.reshape(-1))
    dtb_row = jnp.zeros((1, LANES), F32).at[0, nh:2 * nh].set(dt_bias.reshape(-1))
    row = lambda w: pl.BlockSpec((1, tl, w), lambda bi, i: (bi, i, 0))
    full = lambda a: pl.BlockSpec(a.shape, lambda bi, i: (0,) * a.ndim)
    return pl.pallas_call(
        _dn_prep_body,
        grid=(b, l // tl),
        in_specs=[prev, cur, nxt, row(LANES), full(conv_w), full(alog_row), full(dtb_row)],
        out_specs=[row(BRANCH_W), row(BRANCH_W), row(BRANCH_W), row(LANES)],
        out_shape=[jax.ShapeDtypeStruct((b, l, BRANCH_W), F32)] * 3 + [jax.ShapeDtypeStruct((b, l, LANES), F32)],
        scratch_shapes=[pltpu.VMEM((SUBLANES, tl + 2 * HALO, c), F32)],
        compiler_params=_cparams("parallel", "parallel"),
        name="dn_prep",
    )(u_qkv, u_qkv, u_qkv, u_ba, conv_w, alog_row, dtb_row)


def _dn_chunk_body(q_ref, k_ref, v_ref, gb_ref, uf_ref, wqf_ref, qktf_ref, glf_ref,
                   ub_ref, wqb_ref, qktb_ref, glb_ref):
    C = DN_CHUNK
    nh = 2 * DN_HEADS
    ii = lax.broadcasted_iota(jnp.int32, (C, C), 0)
    jj = lax.broadcasted_iota(jnp.int32, (C, C), 1)
    eye = (ii == jj).astype(F32)
    eye_l = (lax.broadcasted_iota(jnp.int32, (LANES, LANES), 0)
             == lax.broadcasted_iota(jnp.int32, (LANES, LANES), 1)).astype(BF16)
    lower = ((ii >= jj), (ii <= jj))
    strict = ((ii > jj), (ii < jj))
    outs = ((uf_ref, wqf_ref, qktf_ref, glf_ref), (ub_ref, wqb_ref, qktb_ref, glb_ref))
    n_chunks = q_ref.shape[1] // C

    cum = {}
    for ck in range(n_chunks):
        gb = gb_ref[0, ck * C:(ck + 1) * C, :]
        for d in range(2):
            gcum = _dot_exact_lhs(lower[d].astype(BF16), gb)
            hi, mid, lo = _split3(gcum)
            cum[ck, d] = (gb, gcum, _nt(eye_l, hi) + _nt(eye_l, mid) + _nt(eye_l, lo))
            last = C - 1 if d == 0 else 0
            outs[d][3][0, ck * 8:(ck + 1) * 8, :] = jnp.broadcast_to(gcum[last:last + 1, :], (8, LANES))

    units = [(ck, d, hd) for ck in range(n_chunks) for d in range(2) for hd in range(DN_HEADS)]
    kk, kbs, decay, n = {}, {}, {}, {}
    for ck in range(n_chunks):
        for hd in range(DN_HEADS):
            kb = k_ref[0, ck * C:(ck + 1) * C, hd * DN_DK:(hd + 1) * DN_DK].astype(BF16)
            kbs[ck, hd] = kb
            kk[ck, hd] = _nt(kb, kb)
    for (ck, d, hd) in units:
        gb, gcum, gcum_t = cum[ck, d]
        cg = nh + d * DN_HEADS + hd
        cb = d * DN_HEADS + hd
        diff = gcum[:, cg:cg + 1] - gcum_t[cg:cg + 1, :]
        dec = jnp.where(lower[d], jnp.exp(jnp.where(lower[d], diff, 0.0)), 0.0)
        decay[ck, d, hd] = dec
        n[ck, d, hd] = jnp.where(strict[d], kk[ck, hd] * dec * gb[:, cb:cb + 1], 0.0)

    t = {u: eye - n[u] for u in units}
    p = {u: _stacked_dot3([n[u]], n[u])[0] for u in units}
    for _ in range(4):
        tp = {u: _stacked_dot3([t[u], p[u]], p[u]) for u in units}
        t = {u: t[u] + tp[u][0] for u in units}
        p = {u: tp[u][1] for u in units}
    t = {u: t[u] + _stacked_dot3([t[u]], p[u])[0] for u in units}

    for (ck, d, hd) in units:
        gb, gcum, _ = cum[ck, d]
        u_ref, wq_ref, qkt_ref, _ = outs[d]
        cg = nh + d * DN_HEADS + hd
        cb = d * DN_HEADS + hd
        rows = slice(ck * C, (ck + 1) * C)
        cols = slice(hd * DN_DK, (hd + 1) * DN_DK)
        q = q_ref[0, rows, cols]
        k = k_ref[0, rows, cols]
        v = v_ref[0, rows, cols]
        beta = gb[:, cb:cb + 1]
        gc = gcum[:, cg:cg + 1]
        last = C - 1 if d == 0 else 0
        eg = jnp.exp(gc)
        rhs = jnp.concatenate([v * beta, k * (beta * eg)], axis=-1).astype(BF16)
        sol = jnp.dot(t[ck, d, hd].astype(BF16), rhs, preferred_element_type=F32)
        u_ref[0, hd, rows, :] = sol[:, :DN_DV]
        wq_ref[0, hd, ck * 2 * C:ck * 2 * C + C, :] = sol[:, DN_DV:].astype(BF16)
        wq_ref[0, hd, ck * 2 * C + C:(ck + 1) * 2 * C, :] = (q * eg).astype(BF16)
        qk = _nt(q.astype(BF16), kbs[ck, hd]) * decay[ck, d, hd]
        k_dec = (k * jnp.exp(gc[last:last + 1, :] - gc)).astype(BF16)
        qkt_ref[0, hd, ck * 3 * C:ck * 3 * C + C, :] = qk.astype(BF16)
        qkt_ref[0, hd, ck * 3 * C + C:(ck + 1) * 3 * C, :] = _nt(eye_l, k_dec).astype(BF16)


def _dn_chunk(qf, kf, vf, gb):
    b, l, w = qf.shape
    C = DN_CHUNK
    rows = min(DN_CHUNKS_PER_STEP * C, l)
    per = rows // C
    nc = l // C
    row = lambda cw: pl.BlockSpec((1, rows, cw), lambda bi, i: (bi, i, 0))
    head = lambda r, cw: pl.BlockSpec((1, DN_HEADS, r, cw), lambda bi, i: (bi, 0, i, 0))
    one_dir_specs = [head(rows, DN_DV), head(2 * rows, DN_DK), head(3 * rows, C),
                     pl.BlockSpec((1, per * 8, LANES), lambda bi, i: (bi, i, 0))]
    one_dir_shapes = [jax.ShapeDtypeStruct((b, DN_HEADS, l, DN_DV), F32),
                      jax.ShapeDtypeStruct((b, DN_HEADS, 2 * l, DN_DK), BF16),
                      jax.ShapeDtypeStruct((b, DN_HEADS, 3 * l, C), BF16),
                      jax.ShapeDtypeStruct((b, nc * 8, LANES), F32)]
    return pl.pallas_call(
        _dn_chunk_body,
        grid=(b, l // rows),
        in_specs=[row(w), row(w), row(w), row(LANES)],
        out_specs=one_dir_specs * 2,
        out_shape=one_dir_shapes * 2,
        compiler_params=_cparams("parallel", "parallel"),
        name="dn_chunk",
    )(qf, kf, vf, gb)


def _dn_rec_body(uf_ref, wqf_ref, qktf_ref, glf_ref, ub_ref, wqb_ref, qktb_ref, glb_ref, s0f_ref, s0b_ref,
                 of_ref, ob_ref, sf_ref, sb_ref, st_f, st_b):
    c = pl.program_id(1)
    nc = pl.num_programs(1)
    C = DN_CHUNK
    nh = 2 * DN_HEADS

    @pl.when(c == 0)
    def _():
        st_f[...] = s0f_ref[0]
        st_b[...] = s0b_ref[0]

    dirs = ((uf_ref, wqf_ref, qktf_ref, glf_ref, of_ref, st_f), (ub_ref, wqb_ref, qktb_ref, glb_ref, ob_ref, st_b))
    units = [(d, hd) for d in range(2) for hd in range(DN_HEADS)]
    per = uf_ref.shape[2] // C
    s = {(d, hd): dirs[d][5][hd] for (d, hd) in units}
    for step in range(per):
        at = {0: step, 1: per - 1 - step}
        r1 = {(d, hd): jnp.dot(dirs[d][1][0, hd, at[d] * 2 * C:(at[d] + 1) * 2 * C, :], s[d, hd].astype(BF16),
                               preferred_element_type=F32) for (d, hd) in units}
        vb = {(d, hd): (dirs[d][0][0, hd, at[d] * C:(at[d] + 1) * C, :] - r1[d, hd][:C]).astype(BF16)
              for (d, hd) in units}
        r2 = {(d, hd): jnp.dot(dirs[d][2][0, hd, at[d] * 3 * C:(at[d] + 1) * 3 * C, :], vb[d, hd],
                               preferred_element_type=F32) for (d, hd) in units}
        for (d, hd) in units:
            cg = nh + d * DN_HEADS + hd
            decay_all = jnp.exp(dirs[d][3][0, at[d] * 8:at[d] * 8 + 1, cg:cg + 1])
            dirs[d][4][0, at[d] * C:(at[d] + 1) * C, hd * DN_DV:(hd + 1) * DN_DV] = r1[d, hd][C:] + r2[d, hd][:C]
            s[d, hd] = s[d, hd] * decay_all + r2[d, hd][C:]
    for (d, hd) in units:
        dirs[d][5][hd] = s[d, hd]

    @pl.when(c == nc - 1)
    def _():
        sf_ref[0] = st_f[...]
        sb_ref[0] = st_b[...]


def _dn_rec(chunk_outs, s0f, s0b):
    uf = chunk_outs[0]
    b, _, l, _ = uf.shape
    per = min(DN_REC_CHUNKS_PER_STEP, l // DN_CHUNK)
    C = per * DN_CHUNK
    nc = l // C
    fwd = lambda c: c
    bwd = lambda c: nc - 1 - c

    def dir_specs(cidx):
        head = lambda r, cw: pl.BlockSpec((1, DN_HEADS, r, cw), lambda bi, c: (bi, 0, cidx(c), 0))
        return [head(C, DN_DV), head(2 * C, DN_DK), head(3 * C, DN_CHUNK),
                pl.BlockSpec((1, per * 8, LANES), lambda bi, c: (bi, cidx(c), 0))]

    st = pl.BlockSpec((1, DN_HEADS, DN_DK, DN_DV), lambda bi, c: (bi, 0, 0, 0))
    st_shape = jax.ShapeDtypeStruct((b, DN_HEADS, DN_DK, DN_DV), F32)
    out = lambda cidx: pl.BlockSpec((1, C, BRANCH_W), lambda bi, c: (bi, cidx(c), 0))
    return pl.pallas_call(
        _dn_rec_body,
        grid=(b, nc),
        in_specs=dir_specs(fwd) + dir_specs(bwd) + [st, st],
        out_specs=[out(fwd), out(bwd), st, st],
        out_shape=[jax.ShapeDtypeStruct((b, l, BRANCH_W), F32)] * 2 + [st_shape, st_shape],
        scratch_shapes=[pltpu.VMEM((DN_HEADS, DN_DK, DN_DV), F32)] * 2,
        compiler_params=_cparams("parallel", "arbitrary"),
        name="dn_rec",
    )(*chunk_outs, s0f, s0b)


def _att_prep(a, cos_ref, sin_ref, qw_ref, kw_ref, bd_ref, q_ref, kp_ref, vp_ref):
    nq = ATT_HEADS * ATT_HD
    nk = ATT_KV_HEADS * ATT_HD

    def norm_rope(x, w, width):
        ms = jnp.dot((x * x).astype(BF16), bd_ref[:width, :width], preferred_element_type=F32)
        y = x * lax.rsqrt(ms + RMS_EPS) * w
        lane = lax.broadcasted_iota(jnp.int32, y.shape, 1)
        half = ATT_HD // 2
        partner = jnp.where(lane % ATT_HD < half, pltpu.roll(y, width - half, 1), pltpu.roll(y, half, 1))
        return y * cos_ref[:, :width] + partner * sin_ref[:, :width]

    q_ref[0] = norm_rope(a[:, :nq], qw_ref[...], nq).astype(BF16)
    k = norm_rope(a[:, nq:nq + nk], kw_ref[...], nk)
    v = a[:, nq + nk:]
    lane = lax.broadcasted_iota(jnp.int32, k.shape, 1)
    first = lane < ATT_HD
    eye_l = (lax.broadcasted_iota(jnp.int32, (LANES, LANES), 0)
             == lax.broadcasted_iota(jnp.int32, (LANES, LANES), 1)).astype(BF16)
    for x, ref, transposed in ((k, kp_ref, True), (v, vp_ref, False)):
        x0 = jnp.where(first, x, 0.0)
        x1 = jnp.where(first, 0.0, x)
        placed = (x0, pltpu.roll(x0, ATT_HD, 1), pltpu.roll(x1, ATT_HD, 1), x1)
        for j, xp in enumerate(placed):
            xp = xp.astype(BF16)
            ref[0, j] = _nt(eye_l, xp).astype(BF16) if transposed else xp


def _attn_body(q_ref, k_ref, v_ref, o_ref):
    for m in range(ATT_HEADS // 2):
        q = q_ref[0, :, m * LANES:(m + 1) * LANES]
        kv = (2 * m) // (ATT_HEADS // ATT_KV_HEADS)
        acc = None
        for half in range(2):
            s = jnp.dot(q, k_ref[0, 2 * kv + half], preferred_element_type=F32)
            p = jnp.exp(s - jnp.max(s, axis=-1, keepdims=True))
            denom = jnp.sum(p, axis=-1, keepdims=True)
            o = jnp.dot(p.astype(BF16), v_ref[0, 2 * kv + half], preferred_element_type=F32) / denom
            acc = o if acc is None else acc + o
        o_ref[0, :, m * LANES:(m + 1) * LANES] = acc.astype(BF16)


def _attention(q, kp, vp):
    b, l, nq = q.shape
    tq = min(ATT_TQ, l)
    whole = lambda a: pl.BlockSpec((1,) + a.shape[1:], lambda bi, i: (bi, 0, 0, 0))
    return pl.pallas_call(
        _attn_body,
        grid=(b, l // tq),
        in_specs=[pl.BlockSpec((1, tq, nq), lambda bi, i: (bi, i, 0)), whole(kp), whole(vp)],
        out_specs=pl.BlockSpec((1, tq, nq), lambda bi, i: (bi, i, 0)),
        out_shape=jax.ShapeDtypeStruct((b, l, nq), BF16),
        compiler_params=_cparams("parallel", "arbitrary"),
        name="attention",
    )(q, kp, vp)


def _dft_body(ca_ref, sa_ref, cb_ref, sb_ref, pq_ref, o_ref, ctab, stab):
    @pl.when(pl.program_id(1) == 0)
    def _():
        cb = cb_ref[...]
        sb = sb_ref[...]
        for j in range(ca_ref.shape[0]):
            ca = ca_ref[j:j + 1, :]
            sa = sa_ref[j:j + 1, :]
            rows = slice(j * GRID_W, (j + 1) * GRID_W)
            ctab[rows, :] = (ca * cb - sa * sb).astype(BF16)
            stab[rows, :] = (sa * cb + ca * sb).astype(BF16)

    p = pq_ref[0, :, :BRANCH_W]
    q = pq_ref[0, :, BRANCH_W:]
    y = (jnp.dot(ctab[...], p, preferred_element_type=F32)
         - jnp.dot(stab[...], q, preferred_element_type=F32))
    o_ref[0] = y.astype(BF16)


def _seq_dft(pq, tabs):
    ca, sa, cb, sb = tabs
    b, l, w2 = pq.shape
    tn = min(512, l)
    coarse = pl.BlockSpec((tn // GRID_W, l), lambda i, bi: (i, 0))
    fine = pl.BlockSpec((GRID_W, l), lambda i, bi: (0, 0))
    return pl.pallas_call(
        _dft_body,
        grid=(l // tn, b),
        in_specs=[coarse, coarse, fine, fine, pl.BlockSpec((1, l, w2), lambda i, bi: (bi, 0, 0))],
        out_specs=pl.BlockSpec((1, tn, BRANCH_W), lambda i, bi: (bi, i, 0)),
        out_shape=jax.ShapeDtypeStruct((b, l, BRANCH_W), BF16),
        scratch_shapes=[pltpu.VMEM((tn, l), BF16)] * 2,
        compiler_params=_cparams("parallel", "arbitrary"),
        name="seq_dft",
    )(ca, sa, cb, sb, pq)


def _seq_dft_tables(l):
    m = jnp.arange(l, dtype=jnp.int32)[None, :]

    def cs(rows, step, scale):
        k = (jnp.arange(rows, dtype=jnp.int32)[:, None] * step * m) % l
        ang = k.astype(F32) * (2.0 * np.pi / l)
        return jnp.cos(ang) * scale, jnp.sin(ang) * scale

    ca, sa = cs(l // GRID_W, GRID_W, l ** -0.5)
    cb, sb = cs(GRID_W, 1, 1.0)
    return ca, sa, cb, sb


def _dft_tables(n, scale_dtype=BF16):
    k = (jnp.arange(n, dtype=jnp.int32)[:, None] * jnp.arange(n, dtype=jnp.int32)[None, :]) % n
    ang = k.astype(F32) * (2.0 * np.pi / n)
    s = n ** -0.5
    return (jnp.cos(ang) * s).astype(scale_dtype), (jnp.sin(ang) * s).astype(scale_dtype)


def _channel_dft_weight():
    c, s = _dft_tables(FN_GW, F32)
    eye = jnp.eye(FN_GROUPS, dtype=F32)
    return jnp.concatenate([jnp.kron(eye, c), jnp.kron(eye, s)], axis=1).astype(BF16)


def _merge_body(x_ref, of_ref, ob_ref, z_ref, ya_ref, yc_ref, yf_ref, mod_ref, nw_ref, wg_ref, bg_ref,
                wb_ref, wo_ref, lnw_ref, lnb_ref, rw_ref, rb_ref, x1_ref, h2_ref, lg_ref, *, alpha):
    x = x_ref[0]
    sh1, sc1, g1, sh2, sc2 = (mod_ref[0, j:j + 1, :] for j in range(5))
    h = (x * (1.0 + sc1) + sh1).astype(BF16)
    o = of_ref[0] + ob_ref[0]
    z = z_ref[0].astype(F32)
    parts = []
    for hd in range(DN_HEADS):
        cols = slice(hd * DN_DV, (hd + 1) * DN_DV)
        oh = o[:, cols]
        oh = oh * lax.rsqrt(jnp.mean(oh * oh, axis=-1, keepdims=True) + RMS_EPS) * nw_ref[...]
        parts.append(oh * _silu(z[:, cols]))
    y_dn = jnp.concatenate(parts, axis=-1).astype(BF16)
    ys = (y_dn, ya_ref[0], yc_ref[0], yf_ref[0])
    m = None
    for j in range(N_BRANCH):
        cols = slice(j * D_MODEL, (j + 1) * D_MODEL)
        gate = jax.nn.sigmoid(jnp.dot(h, wg_ref[:, cols], preferred_element_type=F32) + bg_ref[:, cols])
        term = gate * jnp.dot(ys[j], wb_ref[j], preferred_element_type=F32)
        m = term if m is None else m + term
    out = jnp.dot(m.astype(BF16), wo_ref[...], preferred_element_type=F32)
    x1 = _ln_rows(alpha * x + g1 * out, lnw_ref[...], lnb_ref[...])
    x1_ref[0] = x1
    h2 = x1 * (1.0 + sc2) + sh2
    h2_ref[0] = h2.astype(BF16)
    lg_ref[0] = _dot3(h2, rw_ref[...]) + rb_ref[...]


def _merge(x, o_f, o_b, z, y_att, y_conf, y_fn, mod, wp, alpha):
    b, l, d = x.shape
    tl = min(MATMUL_ROW_TILE, l)
    row = lambda w: pl.BlockSpec((1, tl, w), lambda bi, i: (bi, i, 0))
    weights = (wp['dn_norm_w'], wp['w_gate'], wp['b_gate'], wp['w_branch'], wp['w_out'],
               wp['ln1_w'], wp['ln1_b'], wp['router_w'], wp['router_b'])
    return pl.pallas_call(
        functools.partial(_merge_body, alpha=alpha),
        grid=(b, l // tl),
        in_specs=[row(d), row(BRANCH_W), row(BRANCH_W), row(BRANCH_W), row(BRANCH_W), row(BRANCH_W),
                  row(BRANCH_W), pl.BlockSpec((1, 8, d), lambda bi, i: (bi, 0, 0))]
                 + [_resident(a) for a in weights],
        out_specs=[row(d), row(d), row(LANES)],
        out_shape=[jax.ShapeDtypeStruct((b, l, d), F32), jax.ShapeDtypeStruct((b, l, d), BF16),
                   jax.ShapeDtypeStruct((b, l, LANES), F32)],
        compiler_params=_cparams("parallel", "parallel"),
        name="merge",
    )(x, o_f, o_b, z, y_att, y_conf, y_fn, mod, *weights)


def _ln2_body(x_ref, y_ref, p_ref, g_ref, w_ref, b_ref, o_ref, *, alpha):
    f = y_ref[0] * p_ref[0]
    for k in range(1, TOP_K):
        f = f + y_ref[k] * p_ref[k]
    o_ref[0] = _ln_rows(alpha * x_ref[0] + g_ref[0] * f, w_ref[...], b_ref[...])


def _ln2(x1, y_slots, p_slots, token_offset, g2, w, bias, alpha):
    b, l, d = x1.shape
    tl = math.gcd(min(ROW_TILE, l), token_offset) if token_offset else min(ROW_TILE, l)
    per_b = l // tl
    off = token_offset // tl
    row = pl.BlockSpec((1, tl, d), lambda bi, i: (bi, i, 0))
    slots = lambda cw: pl.BlockSpec((TOP_K, tl, cw), lambda bi, i: (0, off + bi * per_b + i, 0))
    vec = pl.BlockSpec((1, 1, d), lambda bi, i: (bi, 0, 0))
    par = pl.BlockSpec((1, d), lambda bi, i: (0, 0))
    return pl.pallas_call(
        functools.partial(_ln2_body, alpha=alpha),
        grid=(b, per_b),
        in_specs=[row, slots(d), slots(1), vec, par, par],
        out_specs=row,
        out_shape=jax.ShapeDtypeStruct((b, l, d), F32),
        compiler_params=_cparams("parallel", "parallel"),
        name="ln2",
    )(x1, y_slots, p_slots, g2, w.reshape(1, d), bias.reshape(1, d))


def _moe_ffn_body(tile_ref, exp_ref, lo_ref, hi_ref, x_ref, wgu_ref, bgu_ref, wd_ref, bd_ref, o_ref,
                  wgu_bf, wd_bf):
    w = pl.program_id(0)
    tm = o_ref.shape[0]
    e = exp_ref[w]
    e_prev = exp_ref[jnp.maximum(w - 1, 0)]
    lo = lo_ref[w]
    hi = hi_ref[w]
    base = tile_ref[w] * tm

    @pl.when((w == 0) | (e != e_prev))
    def _():
        wgu_bf[...] = wgu_ref[0, 0].astype(BF16)
        wd_bf[...] = wd_ref[0, 0].astype(BF16)

    whole = (lo == base) & (hi == base + tm)

    @pl.when((lo == base) & jnp.logical_not(whole))
    def _():
        o_ref[...] = jnp.zeros_like(o_ref)

    @pl.when(hi > lo)
    def _():
        packed = x_ref[...]
        x_lo = lax.bitcast_convert_type(packed << 16, F32)
        x_hi = lax.bitcast_convert_type(packed & jnp.uint32(0xFFFF0000), F32)
        x = jnp.concatenate([x_lo, x_hi], axis=-1).astype(BF16)
        gu = jnp.dot(x, wgu_bf[...], preferred_element_type=F32) + bgu_ref[0, 0]
        gate = jnp.minimum(gu[:, :D_EXPERT], SWIGLU_LIMIT)
        up = jnp.clip(gu[:, D_EXPERT:], -SWIGLU_LIMIT, SWIGLU_LIMIT)
        act = (up + 1.0) * gate * jax.nn.sigmoid(SWIGLU_ALPHA * gate)
        y = jnp.dot(act.astype(BF16), wd_bf[...], preferred_element_type=F32) + bd_ref[0, 0]
        @pl.when(whole)
        def _():
            o_ref[...] = y

        @pl.when(jnp.logical_not(whole))
        def _():
            row = base + lax.broadcasted_iota(jnp.int32, (tm, 1), 0)
            o_ref[...] += jnp.where((row >= lo) & (row < hi), y, 0.0)


def _moe_ffn(x_packed, items, layer, w_gu, b_gu, w_down, b_down):
    a, half = x_packed.shape
    d = 2 * half
    tm = MOE_TILE_M
    n_items = items[0].shape[0]
    wmap = lambda w, tile, exp, lo, hi: (layer, exp[w], 0, 0)
    grid_spec = pltpu.PrefetchScalarGridSpec(
        num_scalar_prefetch=4,
        grid=(n_items,),
        in_specs=[
            pl.BlockSpec((tm, half), lambda w, tile, exp, lo, hi: (tile[w], 0)),
            pl.BlockSpec((1, 1, d, 2 * D_EXPERT), wmap),
            pl.BlockSpec((1, 1, 1, 2 * D_EXPERT), wmap),
            pl.BlockSpec((1, 1, D_EXPERT, d), wmap),
            pl.BlockSpec((1, 1, 1, d), wmap),
        ],
        out_specs=pl.BlockSpec((tm, d), lambda w, tile, exp, lo, hi: (tile[w], 0)),
        scratch_shapes=[pltpu.VMEM((d, 2 * D_EXPERT), BF16), pltpu.VMEM((D_EXPERT, d), BF16)],
    )
    depth = w_gu.shape[0]
    return pl.pallas_call(
        _moe_ffn_body,
        grid_spec=grid_spec,
        out_shape=jax.ShapeDtypeStruct((a, d), F32),
        compiler_params=pltpu.CompilerParams(dimension_semantics=("arbitrary",),
                                             vmem_limit_bytes=MOE_VMEM_LIMIT_BYTES),
        name="moe_ffn",
    )(*items, x_packed, w_gu, b_gu.reshape(depth, N_EXPERTS, 1, -1), w_down,
      b_down.reshape(depth, N_EXPERTS, 1, -1))


def _moe(t, logits, layer, w_gu, b_gu, w_down, b_down):
    n, d = t.shape
    a = n * TOP_K
    tm = MOE_TILE_M
    top_v, top_i = lax.top_k(logits, TOP_K)
    probs = jax.nn.softmax(top_v, axis=-1)

    flat_e = top_i.reshape(a).astype(jnp.int32)
    ids = jnp.arange(a, dtype=jnp.int32)
    order = jnp.sort(flat_e * a + ids) % a
    _, pos = lax.sort_key_val(order, ids)
    counts = jnp.sum((flat_e[:, None] == jnp.arange(N_EXPERTS, dtype=jnp.int32)[None, :]).astype(jnp.int32), axis=0)
    ends = jnp.cumsum(counts)

    n_tiles = -(-a // tm)
    cuts =jnp.sort(jnp.concatenate([jnp.arange(n_tiles, dtype=jnp.int32) * tm, ends[:-1].astype(jnp.int32)]))
    item_lo = cuts
    item_hi = jnp.concatenate([cuts[1:], jnp.array([a], jnp.int32)])
    item_tile = jnp.minimum(item_lo // tm, n_tiles - 1)
    item_exp = jnp.minimum(jnp.sum((ends[None, :] <= item_lo[:, None]).astype(jnp.int32), axis=1), N_EXPERTS - 1)

    half = d // 2
    packed = lax.bitcast_convert_type(jnp.stack([t[:, :half], t[:, half:]], axis=-1), jnp.uint32)
    x_sorted = jnp.take(packed, order // TOP_K, axis=0, mode='clip')
    y = _moe_ffn(x_sorted, (item_tile, item_exp, item_lo, item_hi), layer, w_gu, b_gu, w_down, b_down)
    pos_slots = pos.reshape(n, TOP_K).T.reshape(a)
    y_slots = jnp.take(y, pos_slots, axis=0, mode='clip').reshape(TOP_K, n, d)
    return y_slots, probs.T.reshape(TOP_K, n, 1)


def _head_perm():
    return np.concatenate([np.arange(0, ATT_HD, 2), np.arange(1, ATT_HD, 2)])


def _prep_layer_params(lp):
    w = lp['w_in']
    perm = _head_perm()
    qperm = (np.arange(ATT_HEADS)[:, None] * ATT_HD + perm[None, :]).reshape(-1)
    kperm = (np.arange(ATT_KV_HEADS)[:, None] * ATT_HD + perm[None, :]).reshape(-1)
    o = 0
    cols = {}
    for name, width in (('dn', 2048), ('ba', 16), ('aq', 512), ('ak', 128), ('av', 128), ('cu', 1024), ('fu', 512)):
        cols[name] = w[:, o:o + width]
        o += width
    w_cat = jnp.concatenate([
        cols['dn'], jnp.pad(cols['ba'], ((0, 0), (0, LANES - 16))),
        cols['aq'][:, qperm], cols['ak'][:, kperm], cols['av'], cols['cu'], cols['fu']], axis=1).astype(BF16)
    wp = dict(lp)
    wp['w_cat'] = w_cat
    wp['att_qw'] = (jnp.tile(lp['att_q_norm_w'][perm], ATT_HEADS) * (ATT_HD ** -0.5)).reshape(1, -1)
    wp['att_kw'] = jnp.tile(lp['att_k_norm_w'][perm], ATT_KV_HEADS).reshape(1, -1)
    wp['dn_norm_w'] = lp['dn_norm_w'].reshape(1, -1)
    wp['w_gate'] = lp['w_gate'].astype(BF16)
    wp['b_gate'] = lp['b_gate'].reshape(1, -1)
    wp['w_branch'] = lp['w_branch'].astype(BF16)
    wp['w_out'] = lp['w_out'].astype(BF16)
    wp['ln1_w'] = lp['ln1_w'].reshape(1, -1)
    wp['ln1_b'] = lp['ln1_b'].reshape(1, -1)
    wp['router_w'] = jnp.pad(lp['router_w'], ((0, 0), (0, LANES - N_EXPERTS)))
    wp['router_b'] = jnp.pad(lp['router_b'], (0, LANES - N_EXPERTS)).reshape(1, -1)
    return wp


def _rope_tables(l):
    rows = l // GRID_W
    row = jnp.repeat(jnp.arange(rows), GRID_W).astype(F32)
    col = jnp.tile(jnp.arange(GRID_W), rows).astype(F32)
    half = ATT_HD // 2
    inv = ROPE_THETA ** (-jnp.arange(0, half, 2, dtype=F32) / half)
    ang = jnp.concatenate([row[:, None] * inv, col[:, None] * inv], axis=-1)
    cos, sin = jnp.cos(ang), jnp.sin(ang)
    cosf = jnp.tile(jnp.concatenate([cos, cos], axis=-1), (1, ATT_HEADS))
    sinf = jnp.tile(jnp.concatenate([-sin, sin], axis=-1), (1, ATT_HEADS))
    return cosf, sinf


def _adaln(cvec, w, b):
    m = jax.nn.silu(cvec) @ w + b
    return m.reshape(m.shape[:-1] + (6, D_MODEL))


def _mixers(xs, mod, wp, tabs, s0f, s0b, alpha):
    cosf, sinf, bd, w_fn, dft_tabs = tabs
    u_qkv, u_z, u_ba, u_conf, pq, q_rot, kp, vp = _inproj(
        xs, mod[:, 1:2], mod[:, 0:1], wp['w_cat'], w_fn, cosf, sinf, wp['att_qw'], wp['att_kw'], bd)
    qf, kf, vf, gb = _dn_prep(u_qkv, u_ba, wp['dn_conv_w'], wp['dn_a_log'], wp['dn_dt_bias'])
    o_f, o_b, s_f, s_b = _dn_rec(_dn_chunk(qf, kf, vf, gb), s0f, s0b)
    y_conf = _conformer(u_conf, wp['conf_dw_w'], wp['conf_dw_b'], wp['conf_ln_w'], wp['conf_ln_b'])
    y_fn = _seq_dft(pq, dft_tabs)
    return dict(o_f=o_f, o_b=o_b, z=u_z, q=q_rot, kp=kp, vp=vp, y_conf=y_conf, y_fn=y_fn, s_f=s_f, s_b=s_b)


def _trunk_layer(x, cx, c, c_ctx, lp, moe_w, layer, tabs_l, tabs_c, last, alpha):
    b, l, d = x.shape
    lc = cx.shape[1]
    wp = _prep_layer_params(lp)
    mod = _adaln(c, lp['w_ada'], lp['b_ada'])
    mod_c = jnp.broadcast_to(_adaln(c_ctx, lp['w_ada'], lp['b_ada'])[None], (b, 6, d))
    pad8 = lambda m: jnp.pad(m, ((0, 0), (0, 2), (0, 0)))
    s0 = jnp.zeros((b, DN_HEADS, DN_DK, DN_DV), F32)

    mc = _mixers(cx, mod_c, wp, tabs_c, s0, s0, alpha)
    ml = _mixers(x, mod, wp, tabs_l, mc['s_f'], mc['s_b'], alpha)
    kp = jnp.concatenate([mc['kp'], ml['kp']], axis=3)
    vp = jnp.concatenate([mc['vp'], ml['vp']], axis=2)
    y_att = _attention(ml['q'], kp, vp)
    x1, h2, logits = _merge(x, ml['o_f'], ml['o_b'], ml['z'], y_att, ml['y_conf'], ml['y_fn'], pad8(mod), wp, alpha)
    if last:
        y_slots, p_slots = _moe(h2.reshape(-1, d), logits.reshape(-1, LANES)[:, :N_EXPERTS], layer, *moe_w)
        offset = 0
    else:
        yc_att = _attention(mc['q'], mc['kp'], mc['vp'])
        cx1, hc2, logits_c = _merge(cx, mc['o_f'], mc['o_b'], mc['z'], yc_att, mc['y_conf'], mc['y_fn'],
                                    pad8(mod_c), wp, alpha)
        t_all = jnp.concatenate([hc2.reshape(-1, d), h2.reshape(-1, d)], axis=0)
        lg_all = jnp.concatenate([logits_c.reshape(-1, LANES), logits.reshape(-1, LANES)], axis=0)[:, :N_EXPERTS]
        y_slots, p_slots = _moe(t_all, lg_all, layer, *moe_w)
        offset = b * lc
        cx = _ln2(cx1, y_slots, p_slots, 0, mod_c[:, 5:6], lp['ln2_w'], lp['ln2_b'], alpha)
    x = _ln2(x1, y_slots, p_slots, offset, mod[:, 5:6], lp['ln2_w'], lp['ln2_b'], alpha)
    return x, cx


def _tables(l, rope):
    nq = ATT_HEADS * ATT_HD
    if rope:
        cosf, sinf = _rope_tables(l)
    else:
        cosf, sinf = jnp.ones((l, nq), F32), jnp.zeros((l, nq), F32)
    bd = jnp.kron(jnp.eye(ATT_HEADS, dtype=F32), jnp.full((ATT_HD, ATT_HD), 1.0 / ATT_HD, F32)).astype(BF16)
    return cosf, sinf, bd, _channel_dft_weight(), _seq_dft_tables(l)


def kernel(x, c, ctx, c_ctx, w_ada, b_ada, w_in, dn_conv_w, dn_a_log, dn_dt_bias, dn_norm_w, att_q_norm_w, att_k_norm_w, conf_dw_w, conf_dw_b, conf_ln_w, conf_ln_b, w_branch, w_gate, b_gate, w_out, ln1_w, ln1_b, router_w, router_b, exp_w_gu, exp_b_gu, exp_w_down, exp_b_down, ln2_w, ln2_b):
    depth = w_in.shape[0]
    alpha = (2 * depth) ** 0.25
    tabs_l = _tables(x.shape[1], True)
    tabs_c = _tables(ctx.shape[1], False)
    cx = ctx
    for i in range(depth):
        lp = dict(w_ada=w_ada[i], b_ada=b_ada[i], w_in=w_in[i], dn_conv_w=dn_conv_w[i],
                  dn_a_log=dn_a_log[i], dn_dt_bias=dn_dt_bias[i], dn_norm_w=dn_norm_w[i],
                  att_q_norm_w=att_q_norm_w[i], att_k_norm_w=att_k_norm_w[i],
                  conf_dw_w=conf_dw_w[i], conf_dw_b=conf_dw_b[i], conf_ln_w=conf_ln_w[i],
                  conf_ln_b=conf_ln_b[i], w_branch=w_branch[i], w_gate=w_gate[i], b_gate=b_gate[i],
                  w_out=w_out[i], ln1_w=ln1_w[i], ln1_b=ln1_b[i], router_w=router_w[i],
                  router_b=router_b[i], ln2_w=ln2_w[i], ln2_b=ln2_b[i])
        moe_w = (exp_w_gu, exp_b_gu, exp_w_down, exp_b_down)
        x, cx = _trunk_layer(x, cx, c, c_ctx, lp, moe_w, i, tabs_l, tabs_c, i == depth - 1, alpha)
    return x
```
